```python
import math
import jax, jax.numpy as jnp
from jax import lax
import numpy as np

D_MODEL = 1024
BATCH = 2
SEQ = 8192
DEPTH = 4
DEC_BATCH = 128
DEC_SEQ = 8
PAST_LEN = 2048
PAGE_SIZE = 128

N_MIXERS = 3
A_GROUPS = ((128, 1), (512, 4), (2048, 16))
A_HEADS = 8
A_HEAD_DIM = 64
A_WIDTH = A_HEADS * A_HEAD_DIM
ROPE_THETA = 500000.0
ROPE_DIM = A_HEAD_DIM // 4
R_HEADS = 4
R_DK = 256
R_DV = 512
R_THETA = 10000.0
M_HEADS = 4
M_DK = 256
M_DV = 512
CHUNK = 128
D_FF = 4 * D_MODEL
CONV_W = 3
EPS = 1e-6

N_A = len(range(0, DEPTH, N_MIXERS))
N_B = len(range(1, DEPTH, N_MIXERS))
N_C = len(range(2, DEPTH, N_MIXERS))

kernel_name = "hybrid_dilated_retention_mlstm_convffn_step"

F32 = jnp.float32


def rms_norm(x, g):
    xf = x.astype(F32)
    y = xf * lax.rsqrt(jnp.mean(xf * xf, axis=-1, keepdims=True) + EPS)
    return (y * g.astype(F32)).astype(x.dtype)


def head_rms_norm(x, gain=None):
    y = x * lax.rsqrt(jnp.mean(x * x, axis=-1, keepdims=True) + EPS)
    return y if gain is None else y * gain.astype(F32)


def rotary(x, pos, inv_freq):
    n = inv_freq.shape[0]
    ang = pos.astype(F32)[:, None] * inv_freq[None, :]
    bshape = (pos.shape[0],) + (1,) * (x.ndim - 3) + (n,)
    cos = jnp.cos(ang).reshape(bshape)
    sin = jnp.sin(ang).reshape(bshape)
    xr = x[..., :2 * n].astype(F32)
    x1, x2 = xr[..., :n], xr[..., n:]
    rot = jnp.concatenate([x1 * cos - x2 * sin, x2 * cos + x1 * sin], axis=-1).astype(x.dtype)
    return jnp.concatenate([rot, x[..., 2 * n:]], axis=-1)


def _rope_freqs_a():
    return ROPE_THETA ** (-jnp.arange(ROPE_DIM // 2, dtype=F32) * (2.0 / ROPE_DIM))


def _rope_freqs_r():
    return R_THETA ** (-jnp.linspace(0.0, 1.0, R_DK // 2, dtype=F32))


def _softmax_stats(s):
    m = jnp.max(s, axis=-1, keepdims=True)
    p = jnp.exp(s - m)
    l = jnp.sum(p, axis=-1, keepdims=True)
    return p / l, (m + jnp.log(l))[..., 0]


def _proj_a(x, w_in, pos):
    B, T, _ = x.shape
    h = (x @ w_in).reshape(B, T, len(A_GROUPS), 3, A_HEADS, A_HEAD_DIM)
    inv = _rope_freqs_a()
    return rotary(h[:, :, :, 0], pos, inv), rotary(h[:, :, :, 1], pos, inv), h[:, :, :, 2]


def dilated_block_attention(q, k, v, dil, steps):
    B, T, H, dh = q.shape
    n = T // dil
    nb = -(-n // steps)
    npad = nb * steps

    def to_sub(x):
        x = x.reshape(B, n, dil, H, dh).transpose(0, 2, 1, 3, 4)
        x = jnp.pad(x, ((0, 0), (0, 0), (0, npad - n), (0, 0), (0, 0)))
        return x.reshape(B, dil, nb, steps, H, dh)

    def with_prev(x):
        prev = jnp.pad(x, ((0, 0), (0, 0), (1, 0), (0, 0), (0, 0), (0, 0)))[:, :, :-1]
        return jnp.concatenate([prev, x], axis=3)

    qs = to_sub(q)
    kb, vb = with_prev(to_sub(k)), with_prev(to_sub(v))
    qi = jnp.arange(steps)[:, None] + steps
    ki = jnp.arange(2 * steps)[None, :]
    dist = qi - ki
    blk = jnp.arange(nb)[:, None, None]
    valid = ((dist >= 0) & (dist <= steps))[None] & ((blk * steps - steps + ki[None]) >= 0)
    s = jnp.einsum('brnqhd,brnkhd->brnhqk', qs, kb, preferred_element_type=F32) * (dh ** -0.5)
    s = jnp.where(valid[None, None, :, None], s, -jnp.inf)
    p, lse = _softmax_stats(s)
    o = jnp.einsum('brnhqk,brnkhd->brnqhd', p.astype(vb.dtype), vb)
    o = o.reshape(B, dil, npad, H, dh)[:, :, :n].transpose(0, 2, 1, 3, 4).reshape(B, T, H, dh)
    lse = lse.transpose(0, 1, 2, 4, 3).reshape(B, dil, npad, H)[:, :, :n]
    lse = lse.transpose(0, 2, 1, 3).reshape(B, T, H)
    return o, lse


def dilated_gather_attention(q, kc, vc, n_past, dil, steps):
    Bd, S, H, dh = q.shape
    idx = n_past + jnp.arange(S)[:, None] - dil * jnp.arange(steps + 1)[None, :]
    valid = idx >= 0
    flat = jnp.clip(idx, 0).reshape(-1)
    kg = jnp.take(kc, flat, axis=1).reshape(Bd, S, steps + 1, H, dh)
    vg = jnp.take(vc, flat, axis=1).reshape(Bd, S, steps + 1, H, dh)
    s = jnp.einsum('bqhd,bqjhd->bhqj', q, kg, preferred_element_type=F32) * (dh ** -0.5)
    s = jnp.where(valid[None, None], s, -jnp.inf)
    p, lse = _softmax_stats(s)
    o = jnp.einsum('bhqj,bqjhd->bqhd', p.astype(vg.dtype), vg)
    return o, lse.transpose(0, 2, 1)


def _merge_a(outs, lses, w_out):
    wts = jax.nn.softmax(jnp.stack(lses, axis=0), axis=0)
    o = jnp.einsum('gbth,gbthd->bthd', wts, jnp.stack(outs, axis=0).astype(F32))
    B, T = o.shape[:2]
    return o.reshape(B, T, A_WIDTH).astype(w_out.dtype) @ w_out


def mixer_a_prompt(x, w_in, w_out, pos):
    q, k, v = _proj_a(x, w_in, pos)
    T = x.shape[1]
    outs, lses, rows = [], [], []
    for g, (win, dil) in enumerate(A_GROUPS):
        o, l = dilated_block_attention(q[:, :, g], k[:, :, g], v[:, :, g], dil, win // dil)
        outs.append(o)
        lses.append(l)
        keep = min(win, T)
        rows.append(jnp.stack([k[:, T - keep:, g], v[:, T - keep:, g]], axis=2))
    return _merge_a(outs, lses, w_out), rows


def mixer_a_sample(x, bufs, w_in, w_out, pos):
    q, k, v = _proj_a(x, w_in, pos)
    outs, lses, rows = [], [], []
    for g, ((win, dil), buf) in enumerate(zip(A_GROUPS, bufs)):
        kc = jnp.concatenate([buf[:, :, 0].astype(k.dtype), k[:, :, g]], axis=1)
        vc = jnp.concatenate([buf[:, :, 1].astype(v.dtype), v[:, :, g]], axis=1)
        o, l = dilated_gather_attention(q[:, :, g], kc, vc, buf.shape[1], dil, win // dil)
        outs.append(o)
        lses.append(l)
        rows.append(jnp.stack([k[:, :, g], v[:, :, g]], axis=2))
    return _merge_a(outs, lses, w_out), rows


def _to_chunks(x, L):
    B, T = x.shape[:2]
    return jnp.moveaxis(x.reshape((B, T // L, L) + x.shape[2:]), 1, 0)


def _from_chunks(x):
    nc, B, L = x.shape[:3]
    return jnp.moveaxis(x, 0, 1).reshape((B, nc * L) + x.shape[3:])


def retention_chunks(q, k, v, log_g, state):
    T = q.shape[1]
    L = CHUNK if T % CHUNK == 0 else T
    q, k, v = q.astype(F32), k.astype(F32), v.astype(F32)
    i = jnp.arange(L, dtype=F32)
    diff = i[:, None] - i[None, :]
    decay = jnp.where(diff >= 0, jnp.exp(jnp.maximum(diff, 0.0)[None] * log_g[:, None, None]), 0.0)
    inner = jnp.exp((i + 1.0)[:, None] * log_g[None, :])[None, :, :, None]
    tail = jnp.exp((L - 1.0 - i)[:, None] * log_g[None, :])[None, :, :, None]
    chunk_decay = jnp.exp(L * log_g)[None, :, None, None]

    def step(S, inp):
        qc, kc, vc = inp
        s = jnp.einsum('bihd,bjhd->bhij', qc, kc) * decay
        o = jnp.einsum('bhij,bjhe->bihe', s, vc) + jnp.einsum('bihd,bhde->bihe', qc, S) * inner
        S = S * chunk_decay + jnp.einsum('bjhd,bjhe->bhde', kc * tail, vc)
        return S, o

    S, o = lax.scan(step, state.astype(F32), (_to_chunks(q, L), _to_chunks(k, L), _to_chunks(v, L)))
    return _from_chunks(o), S


def mixer_b(x, state, w_in, w_out, pos):
    B, T, _ = x.shape
    nq, nv = R_HEADS * R_DK, R_HEADS * R_DV
    q, k, v, g = jnp.split(x @ w_in, [nq, 2 * nq, 2 * nq + nv], axis=-1)
    inv = _rope_freqs_r()
    q = rotary(q.reshape(B, T, R_HEADS, R_DK), pos, inv)
    k = rotary(k.reshape(B, T, R_HEADS, R_DK), pos, inv) * (R_DK ** -0.5)
    v = v.reshape(B, T, R_HEADS, R_DV)
    log_g = jnp.log1p(-jnp.exp2(-5.0 - jnp.arange(R_HEADS, dtype=F32)))
    o, new_state = retention_chunks(q, k, v, log_g, state)
    o = head_rms_norm(o).reshape(B, T, nv)
    y = (jax.nn.silu(g.astype(F32)) * o).astype(x.dtype) @ w_out
    return y, new_state


def mlstm_chunks(q, k, v, i_pre, log_f, C, n, m):
    T = q.shape[1]
    L = CHUNK if T % CHUNK == 0 else T
    q, k, v = q.astype(F32), k.astype(F32), v.astype(F32)
    idx = jnp.arange(L)
    causal = (idx[:, None] >= idx[None, :])[None, :, :, None]

    def step(carry, inp):
        C, n, m = carry
        qc, kc, vc, ic, fc = inp
        b = jnp.cumsum(fc, axis=1)
        logw = jnp.where(causal, b[:, :, None, :] - b[:, None, :, :] + ic[:, None, :, :], -jnp.inf)
        inter = b + m[:, None, :]
        m_t = jnp.maximum(inter, jnp.max(logw, axis=2))
        w = jnp.exp(logw - m_t[:, :, None, :])
        a = jnp.exp(inter - m_t)
        qk = jnp.einsum('bthd,bshd->btsh', qc, kc) * w
        num = jnp.einsum('btsh,bshe->bthe', qk, vc) + a[..., None] * jnp.einsum('bthd,bhde->bthe', qc, C)
        den = jnp.sum(qk, axis=2) + a * jnp.einsum('bthd,bhd->bth', qc, n)
        h = num / jnp.maximum(jnp.abs(den), jnp.exp(-m_t))[..., None]
        a_end, w_end = a[:, -1], w[:, -1]
        C = a_end[..., None, None] * C + jnp.einsum('bshd,bshe->bhde', kc * w_end[..., None], vc)
        n = a_end[..., None] * n + jnp.einsum('bsh,bshd->bhd', w_end, kc)
        return (C, n, m_t[:, -1]), h

    init = (C.astype(F32), n.astype(F32), m.astype(F32))
    xs = (_to_chunks(q, L), _to_chunks(k, L), _to_chunks(v, L), _to_chunks(i_pre, L), _to_chunks(log_f, L))
    (C, n, m), h = lax.scan(step, init, xs)
    return _from_chunks(h), C, n, m


def mixer_c(x, C, n, m, w_in, b_if, norm_g, w_out):
    B, T, _ = x.shape
    nq, nv = M_HEADS * M_DK, M_HEADS * M_DV
    q, k, v, o, ig, fg = jnp.split(x @ w_in, [nq, 2 * nq, 2 * nq + nv, 2 * nq + 2 * nv, 2 * nq + 2 * nv + M_HEADS], axis=-1)
    q = q.reshape(B, T, M_HEADS, M_DK)
    k = k.reshape(B, T, M_HEADS, M_DK) * (M_DK ** -0.5)
    v = v.reshape(B, T, M_HEADS, M_DV)
    i_pre = ig.astype(F32) + b_if[0].astype(F32)
    log_f = jax.nn.log_sigmoid(fg.astype(F32) + b_if[1].astype(F32))
    h, C, n, m = mlstm_chunks(q, k, v, i_pre, log_f, C, n, m)
    h = head_rms_norm(h, norm_g).reshape(B, T, nv)
    y = (jax.nn.sigmoid(o.astype(F32)) * h).astype(x.dtype) @ w_out
    return y, C, n, m


def conv_ffn(x, buf, w_up, conv_w, conv_b, w_down):
    T = x.shape[1]
    a, b = jnp.split(x @ w_up, 2, axis=-1)
    ac = jnp.concatenate([buf.astype(a.dtype), a], axis=1)
    conv = conv_b + sum(ac[:, j:j + T] * conv_w[j] for j in range(CONV_W))
    y = (jax.nn.gelu(conv.astype(F32), approximate=True) * b.astype(F32)).astype(x.dtype) @ w_down
    return y, ac[:, T:]


def setup_inputs(seed: int = 0) -> dict:
    key = jax.random.key(seed)
    keys = iter(jax.random.split(key, 32))

    def normal(shape, scale):
        return jax.random.normal(next(keys), shape, jnp.float32) * scale

    d = D_MODEL
    n_in_a = len(A_GROUPS) * 3 * A_WIDTH
    n_in_b = 2 * R_HEADS * R_DK + 2 * R_HEADS * R_DV
    n_in_c = 2 * M_HEADS * M_DK + 2 * M_HEADS * M_DV + 2 * M_HEADS
    forget_bias = jnp.linspace(3.0, 6.0, M_HEADS, dtype=jnp.float32)
    win_shape = lambda w: (N_A, DEC_BATCH, min(w, PAST_LEN), 2, A_HEADS, A_HEAD_DIM)
    return {
        "x_prompt": normal((BATCH, SEQ, d), 1.0),
        "x_sample": normal((DEC_BATCH, DEC_SEQ, d), 1.0),
        "cache_win128_kv": normal(win_shape(A_GROUPS[0][0]), 1.0),
        "cache_win512_kv": normal(win_shape(A_GROUPS[1][0]), 1.0),
        "cache_win2048_kv": normal(win_shape(A_GROUPS[2][0]), 1.0),
        "state_ret": normal((N_B, DEC_BATCH, R_HEADS, R_DK, R_DV), 0.5),
        "state_mlstm_c": normal((N_C, DEC_BATCH, M_HEADS, M_DK, M_DV), 0.1),
        "state_mlstm_n": normal((N_C, DEC_BATCH, M_HEADS, M_DK), 0.1),
        "state_mlstm_m": normal((N_C, DEC_BATCH, M_HEADS), 0.5),
        "state_ffn_conv": normal((DEPTH, DEC_BATCH, CONV_W - 1, D_FF), 1.0),
        "norm_gains": 1.0 + normal((DEPTH, 4, d), 0.02),
        "a_w_in": normal((N_A, d, n_in_a), d ** -0.5),
        "a_w_out": normal((N_A, A_WIDTH, d), A_WIDTH ** -0.5),
        "b_w_in": normal((N_B, d, n_in_b), d ** -0.5),
        "b_w_out": normal((N_B, R_HEADS * R_DV, d), (R_HEADS * R_DV) ** -0.5),
        "c_w_in": normal((N_C, d, n_in_c), d ** -0.5),
        "c_b_if": jnp.stack([normal((N_C, M_HEADS), 0.1), forget_bias + normal((N_C, M_HEADS), 0.1)], axis=1),
        "c_norm": 1.0 + normal((N_C, M_HEADS, M_DV), 0.02),
        "c_w_out": normal((N_C, M_HEADS * M_DV, d), (M_HEADS * M_DV) ** -0.5),
        "f_w_up": normal((DEPTH, d, 2 * D_FF), d ** -0.5),
        "f_conv_w": normal((DEPTH, CONV_W, D_FF), CONV_W ** -0.5),
        "f_conv_b": normal((DEPTH, D_FF), 0.01),
        "f_w_down": normal((DEPTH, D_FF, d), D_FF ** -0.5),
    }


def reference(x_prompt, x_sample, cache_win128_kv, cache_win512_kv, cache_win2048_kv, state_ret,
              state_mlstm_c, state_mlstm_n, state_mlstm_m, state_ffn_conv, norm_gains,
              a_w_in, a_w_out, b_w_in, b_w_out, c_w_in, c_b_if, c_norm, c_w_out,
              f_w_up, f_conv_w, f_conv_b, f_w_down):
    Bp, T = x_prompt.shape[:2]
    Bs, S = x_sample.shape[:2]
    pos_p = jnp.arange(T, dtype=jnp.int32)
    pos_s = PAST_LEN + jnp.arange(S, dtype=jnp.int32)
    win_caches = (cache_win128_kv, cache_win512_kv, cache_win2048_kv)
    win_p = [[] for _ in A_GROUPS]
    win_s = [[] for _ in A_GROUPS]
    ret_pl, ret_sl = [], []
    c_pl, c_sl, n_pl, n_sl, m_pl, m_sl = [], [], [], [], [], []
    conv_pl, conv_sl = [], []
    xp, xs = x_prompt, x_sample
    for layer in range(DEPTH):
        kind, j = layer % N_MIXERS, layer // N_MIXERS
        g = norm_gains[layer]
        hp, hs = rms_norm(xp, g[0]), rms_norm(xs, g[0])
        if kind == 0:
            yp, rows_p = mixer_a_prompt(hp, a_w_in[j], a_w_out[j], pos_p)
            ys, rows_s = mixer_a_sample(hs, [c[j] for c in win_caches], a_w_in[j], a_w_out[j], pos_s)
            for gi in range(len(A_GROUPS)):
                win_p[gi].append(rows_p[gi])
                win_s[gi].append(rows_s[gi])
        elif kind == 1:
            zero_s = jnp.zeros((Bp, R_HEADS, R_DK, R_DV), F32)
            yp, sp = mixer_b(hp, zero_s, b_w_in[j], b_w_out[j], pos_p)
            ys, ss = mixer_b(hs, state_ret[j], b_w_in[j], b_w_out[j], pos_s)
            ret_pl.append(sp)
            ret_sl.append(ss)
        else:
            zc = jnp.zeros((Bp, M_HEADS, M_DK, M_DV), F32)
            zn = jnp.zeros((Bp, M_HEADS, M_DK), F32)
            zm = jnp.zeros((Bp, M_HEADS), F32)
            yp, cp, npp, mp = mixer_c(hp, zc, zn, zm, c_w_in[j], c_b_if[j], c_norm[j], c_w_out[j])
            ys, cs, nss, ms = mixer_c(hs, state_mlstm_c[j], state_mlstm_n[j], state_mlstm_m[j],
                                      c_w_in[j], c_b_if[j], c_norm[j], c_w_out[j])
            c_pl.append(cp); c_sl.append(cs)
            n_pl.append(npp); n_sl.append(nss)
            m_pl.append(mp); m_sl.append(ms)
        xp = xp + rms_norm(yp, g[1])
        xs = xs + rms_norm(ys, g[1])
        zero_buf = jnp.zeros((Bp, CONV_W - 1, D_FF), xp.dtype)
        fp, bp = conv_ffn(rms_norm(xp, g[2]), zero_buf, f_w_up[layer], f_conv_w[layer], f_conv_b[layer], f_w_down[layer])
        fs, bs = conv_ffn(rms_norm(xs, g[2]), state_ffn_conv[layer], f_w_up[layer], f_conv_w[layer], f_conv_b[layer], f_w_down[layer])
        conv_pl.append(bp)
        conv_sl.append(bs)
        xp = xp + rms_norm(fp, g[3])
        xs = xs + rms_norm(fs, g[3])
    y_prompt, y_sample = xp, xs
    win128_p, win128_s = jnp.stack(win_p[0]), jnp.stack(win_s[0])
    win512_p, win512_s = jnp.stack(win_p[1]), jnp.stack(win_s[1])
    win2048_p, win2048_s = jnp.stack(win_p[2]), jnp.stack(win_s[2])
    ret_p, ret_s = jnp.stack(ret_pl), jnp.stack(ret_sl)
    mlstm_c_p, mlstm_c_s = jnp.stack(c_pl), jnp.stack(c_sl)
    mlstm_n_p, mlstm_n_s = jnp.stack(n_pl), jnp.stack(n_sl)
    mlstm_m_p, mlstm_m_s = jnp.stack(m_pl), jnp.stack(m_sl)
    conv_p, conv_s = jnp.stack(conv_pl), jnp.stack(conv_sl)
    return (y_prompt, y_sample, win128_p, win128_s, win512_p, win512_s, win2048_p, win2048_s,
            ret_p, ret_s, mlstm_c_p, mlstm_c_s, mlstm_n_p, mlstm_n_s, mlstm_m_p, mlstm_m_s, conv_p, conv_s)
```

```python
import functools

import jax
import jax.numpy as jnp
from jax import lax
from jax.experimental import pallas as pl
from jax.experimental.pallas import tpu as pltpu

F32 = jnp.float32
BF16 = jnp.bfloat16

D_MODEL = 1024
PAST_LEN = 2048
N_MIXERS = 3
A_GROUPS = ((128, 1), (512, 4), (2048, 16))
A_HEADS = 8
A_HEAD_DIM = 64
A_WIDTH = A_HEADS * A_HEAD_DIM
A_STEPS = 128
ROPE_THETA = 500000.0
ROPE_DIM = A_HEAD_DIM // 4
R_HEADS = 4
R_DK = 256
R_DV = 512
R_THETA = 10000.0
M_HEADS = 4
M_DK = 256
M_DV = 512
CHUNK = 128
D_FF = 4 * D_MODEL
EPS = 1e-6

LANES = 128
BF16_ROWS = 16
VMEM_LIMIT = 52 * 1024 * 1024

NT_DIMS = (((1,), (1,)), ((), ()))
TN_DIMS = (((0,), (0,)), ((), ()))
NEG_INF = float("-inf")


def _cparams(*sem):
    return pltpu.CompilerParams(dimension_semantics=sem, vmem_limit_bytes=VMEM_LIMIT)


def _rms(x):
    return x * lax.rsqrt(jnp.mean(x * x, axis=-1, keepdims=True) + EPS)


def _dot(a, b):
    return jnp.dot(a, b, preferred_element_type=F32)


def _dot_nt(a, b):
    return lax.dot_general(a, b, NT_DIMS, preferred_element_type=F32)


def _dot_tn(a, b):
    return lax.dot_general(a, b, TN_DIMS, preferred_element_type=F32)


def _pad_rows(x, rows):
    if x.shape[0] == rows:
        return x
    return jnp.concatenate([x, jnp.zeros((rows - x.shape[0], x.shape[1]), x.dtype)], axis=0)


def _norm_proj_kernel(*refs, mode):
    if mode == "rope_a":
        x_ref, g_ref, w_ref, c_ref, sp_ref, sm_ref, o_ref, h_scr = refs
    elif mode == "rope_b":
        x_ref, g_ref, w_ref, c_ref, s_ref, o_ref, h_scr = refs
    else:
        x_ref, g_ref, w_ref, wg_ref, o_ref, og_ref, h_scr = refs
    j = pl.program_id(1)

    @pl.when(j == 0)
    def _():
        h_scr[...] = (_rms(x_ref[...]) * g_ref[...]).astype(BF16)
        if mode == "gates":
            og_ref[...] = _dot(h_scr[...], wg_ref[...])

    acc = _dot(h_scr[...], w_ref[...])
    if mode == "rope_a":
        @pl.when(j % 3 != 2)
        def _():
            reps = acc.shape[1] // LANES
            c = jnp.concatenate([c_ref[...]] * reps, axis=1)
            sp = jnp.concatenate([sp_ref[...]] * reps, axis=1)
            sm = jnp.concatenate([sm_ref[...]] * reps, axis=1)
            half = ROPE_DIM // 2
            up = pltpu.roll(acc, half, 1)
            down = pltpu.roll(acc, acc.shape[1] - half, 1)
            o_ref[...] = acc * c + down * sm + up * sp

        @pl.when(j % 3 == 2)
        def _():
            o_ref[...] = acc
    elif mode == "rope_b":
        @pl.when(j < 4)
        def _():
            c = c_ref[...]
            s = s_ref[...]
            outs = []
            for hh in range(acc.shape[1] // R_DK):
                x1 = acc[:, hh * R_DK: hh * R_DK + R_DK // 2]
                x2 = acc[:, hh * R_DK + R_DK // 2: (hh + 1) * R_DK]
                outs += [x1 * c - x2 * s, x2 * c + x1 * s]
            o_ref[...] = jnp.concatenate(outs, axis=1)

        @pl.when(j >= 4)
        def _():
            o_ref[...] = acc
    else:
        o_ref[...] = acc


def _norm_proj(x, gain, w, *, mode, tabs=(), tab_blocks=1, wg=None, tm=512, tn=512):
    n, d = x.shape
    m = w.shape[1]
    grid = (n // tm, m // tn)
    in_specs = [
        pl.BlockSpec((tm, d), lambda i, j: (i, 0)),
        pl.BlockSpec((1, d), lambda i, j: (0, 0)),
        pl.BlockSpec((d, tn), lambda i, j: (0, j)),
    ]
    args = [x, gain.reshape(1, d), w]
    out_shape = [jax.ShapeDtypeStruct((n, m), F32)]
    out_specs = [pl.BlockSpec((tm, tn), lambda i, j: (i, j))]
    if mode in ("rope_a", "rope_b"):
        for t in tabs:
            in_specs.append(pl.BlockSpec((tm, LANES), lambda i, j: (i % tab_blocks, 0)))
            args.append(t)
    else:
        in_specs.append(pl.BlockSpec((d, LANES), lambda i, j: (0, 0)))
        args.append(wg)
        out_shape.append(jax.ShapeDtypeStruct((n, LANES), F32))
        out_specs.append(pl.BlockSpec((tm, LANES), lambda i, j: (i, 0)))
    outs = pl.pallas_call(
        functools.partial(_norm_proj_kernel, mode=mode),
        out_shape=out_shape,
        grid=grid,
        in_specs=in_specs,
        out_specs=out_specs,
        scratch_shapes=[pltpu.VMEM((tm, d), BF16)],
        compiler_params=_cparams("parallel", "arbitrary"),
        name="norm_proj_" + mode,
    )(*args)
    return outs if mode == "gates" else outs[0]


def _post_kernel(*refs, merge):
    if merge:
        o0, o1, o2, l0, l1, l2, w_ref, x_ref, g_ref, out_ref = refs
        la, lb, lc = l0[...], l1[...], l2[...]
        mx = jnp.maximum(jnp.maximum(la, lb), lc)
        ea, eb, ec = jnp.exp(la - mx), jnp.exp(lb - mx), jnp.exp(lc - mx)
        den = ea + eb + ec
        a = (ea / den) * o0[...] + (eb / den) * o1[...] + (ec / den) * o2[...]
    else:
        a_ref, w_ref, x_ref, g_ref, out_ref = refs
        a = a_ref[...]
    y = _dot(a.astype(BF16), w_ref[...])
    out_ref[...] = x_ref[...] + _rms(y) * g_ref[...]


def _post(a_list, w, x, gain, *, merge, tm=512):
    n, d = x.shape
    k = w.shape[0]
    in_specs = [pl.BlockSpec((tm, k), lambda i: (i, 0)) for _ in a_list]
    in_specs += [
        pl.BlockSpec((k, d), lambda i: (0, 0)),
        pl.BlockSpec((tm, d), lambda i: (i, 0)),
        pl.BlockSpec((1, d), lambda i: (0, 0)),
    ]
    return pl.pallas_call(
        functools.partial(_post_kernel, merge=merge),
        out_shape=jax.ShapeDtypeStruct((n, d), F32),
        grid=(n // tm,),
        in_specs=in_specs,
        out_specs=pl.BlockSpec((tm, d), lambda i: (i, 0)),
        compiler_params=_cparams("parallel"),
        name="post_merge" if merge else "post",
    )(*a_list, w, x, gain.reshape(1, d))


def _ffn_tail(conv, b, wd_ref, acc_scr, x_ref, g3_ref, o_ref):
    j = pl.program_id(1)
    act = jax.nn.gelu(conv, approximate=True) * b
    acc_scr[...] += _dot(act.astype(BF16), wd_ref[...])

    @pl.when(j == pl.num_programs(1) - 1)
    def _():
        o_ref[...] = x_ref[...] + _rms(acc_scr[...]) * g3_ref[...]


def _ffn_prompt_kernel(x_ref, halo_ref, g2_ref, g3_ref, wa_ref, wb_ref, cw_ref, cb_ref, wd_ref,
                       o_ref, tail_ref, h_scr, acc_scr, *, tiles_per_seq):
    i = pl.program_id(0)
    j = pl.program_id(1)
    hr = halo_ref.shape[0]
    tm = x_ref.shape[0]

    @pl.when(j == 0)
    def _():
        keep = jnp.where(i % tiles_per_seq != 0, 1.0, 0.0)
        h_scr[:hr, :] = (_rms(halo_ref[...]) * g2_ref[...] * keep).astype(BF16)
        h_scr[hr:, :] = (_rms(x_ref[...]) * g2_ref[...]).astype(BF16)
        acc_scr[...] = jnp.zeros_like(acc_scr)

    a_ext = _dot(h_scr[...], wa_ref[...])
    b = _dot(h_scr[hr:, :], wb_ref[...])
    a = a_ext[hr:, :]
    prev1 = a_ext[hr - 1:hr, :]
    prev2 = a_ext[hr - 2:hr - 1, :]
    row = lax.broadcasted_iota(jnp.int32, a.shape, 0)
    r1 = jnp.where(row == 0, prev1, pltpu.roll(a, 1, 0))
    r2 = jnp.where(row == 0, prev2, jnp.where(row == 1, prev1, pltpu.roll(a, 2, 0)))
    cw = cw_ref[...]
    conv = cb_ref[...] + ((r2 * cw[0:1, :] + r1 * cw[1:2, :]) + a * cw[2:3, :])
    tail_ref[...] = a[tm - 8:, :]
    _ffn_tail(conv, b, wd_ref, acc_scr, x_ref, g3_ref, o_ref)


def _ffn_sample_kernel(x_ref, p1_ref, p2_ref, g2_ref, g3_ref, wa_ref, wb_ref, cw_ref, cb_ref, wd_ref,
                       o_ref, a_ref, h_scr, acc_scr, *, seq):
    j = pl.program_id(1)

    @pl.when(j == 0)
    def _():
        h_scr[...] = (_rms(x_ref[...]) * g2_ref[...]).astype(BF16)
        acc_scr[...] = jnp.zeros_like(acc_scr)

    a = _dot(h_scr[...], wa_ref[...])
    b = _dot(h_scr[...], wb_ref[...])
    pos = lax.broadcasted_iota(jnp.int32, a.shape, 0) & (seq - 1)
    r1 = jnp.where(pos == 0, p1_ref[...], pltpu.roll(a, 1, 0))
    r2 = jnp.where(pos < 2, p2_ref[...], pltpu.roll(a, 2, 0))
    cw = cw_ref[...]
    conv = cb_ref[...] + ((r2 * cw[0:1, :] + r1 * cw[1:2, :]) + a * cw[2:3, :])
    a_ref[...] = a
    _ffn_tail(conv, b, wd_ref, acc_scr, x_ref, g3_ref, o_ref)


def _ffn_common_specs(d, dff, tf):
    nf = dff // tf
    return [
        pl.BlockSpec((1, d), lambda i, j: (0, 0)),
        pl.BlockSpec((1, d), lambda i, j: (0, 0)),
        pl.BlockSpec((d, tf), lambda i, j: (0, j)),
        pl.BlockSpec((d, tf), lambda i, j: (0, j + nf)),
        pl.BlockSpec((3, tf), lambda i, j: (0, j)),
        pl.BlockSpec((1, tf), lambda i, j: (0, j)),
        pl.BlockSpec((tf, d), lambda i, j: (j, 0)),
    ]


def _ffn_prompt(x, seq_len, g2, g3, w_up, conv_w, conv_b, w_down, *, tm=512, tf=512):
    n, d = x.shape
    dff = w_down.shape[0]
    nt = n // tm
    hr = BF16_ROWS
    in_specs = [
        pl.BlockSpec((tm, d), lambda i, j: (i, 0)),
        pl.BlockSpec((hr, d), lambda i, j: (jnp.maximum(i * (tm // hr) - 1, 0), 0)),
    ] + _ffn_common_specs(d, dff, tf)
    y, tail = pl.pallas_call(
        functools.partial(_ffn_prompt_kernel, tiles_per_seq=seq_len // tm),
        out_shape=[jax.ShapeDtypeStruct((n, d), F32), jax.ShapeDtypeStruct((nt * 8, dff), F32)],
        grid=(nt, dff // tf),
        in_specs=in_specs,
        out_specs=[pl.BlockSpec((tm, d), lambda i, j: (i, 0)), pl.BlockSpec((8, tf), lambda i, j: (i, j))],
        scratch_shapes=[pltpu.VMEM((tm + hr, d), BF16), pltpu.VMEM((tm, d), F32)],
        compiler_params=_cparams("parallel", "arbitrary"),
        name="ffn_prompt",
    )(x, x, g2.reshape(1, d), g3.reshape(1, d), w_up, w_up, conv_w, conv_b.reshape(1, dff), w_down)
    return y, tail


def _ffn_sample(x, seq, p1, p2, g2, g3, w_up, conv_w, conv_b, w_down, *, tm=512, tf=512):
    n, d = x.shape
    dff = w_down.shape[0]
    in_specs = [
        pl.BlockSpec((tm, d), lambda i, j: (i, 0)),
        pl.BlockSpec((tm, tf), lambda i, j: (i, j)),
        pl.BlockSpec((tm, tf), lambda i, j: (i, j)),
    ] + _ffn_common_specs(d, dff, tf)
    y, a = pl.pallas_call(
        functools.partial(_ffn_sample_kernel, seq=seq),
        out_shape=[jax.ShapeDtypeStruct((n, d), F32), jax.ShapeDtypeStruct((n, dff), F32)],
        grid=(n // tm, dff // tf),
        in_specs=in_specs,
        out_specs=[pl.BlockSpec((tm, d), lambda i, j: (i, 0)), pl.BlockSpec((tm, tf), lambda i, j: (i, j))],
        scratch_shapes=[pltpu.VMEM((tm, d), BF16), pltpu.VMEM((tm, d), F32)],
        compiler_params=_cparams("parallel", "arbitrary"),
        name="ffn_sample",
    )(x, p1, p2, g2.reshape(1, d), g3.reshape(1, d), w_up, w_up, conv_w, conv_b.reshape(1, dff), w_down)
    return y, a


def _attn_prompt_kernel(q_ref, kp_ref, kc_ref, vp_ref, vc_ref, o_ref, l_ref):
    i = pl.program_id(2)
    tq = q_ref.shape[0]
    qi = lax.broadcasted_iota(jnp.int32, (tq, 2 * tq), 0) + tq
    ki = lax.broadcasted_iota(jnp.int32, (tq, 2 * tq), 1)
    dist = qi - ki
    first_key = jnp.where(i > 0, 0, tq)
    valid = (dist >= 0) & (dist <= A_STEPS) & (ki >= first_key)
    scale = A_HEAD_DIM ** -0.5
    for h in range(A_HEADS):
        hs = slice(h * A_HEAD_DIM, (h + 1) * A_HEAD_DIM)
        q = q_ref[:, hs].astype(BF16)
        k = jnp.concatenate([kp_ref[:, hs], kc_ref[:, hs]], axis=0).astype(BF16)
        v = jnp.concatenate([vp_ref[:, hs], vc_ref[:, hs]], axis=0).astype(BF16)
        s = jnp.where(valid, _dot_nt(q, k) * scale, NEG_INF)
        m = jnp.max(s, axis=-1, keepdims=True)
        p = jnp.exp(s - m)
        l = jnp.sum(p, axis=-1, keepdims=True)
        o_ref[:, hs] = _dot(p.astype(BF16), v) / l
        l_ref[:, hs] = jnp.broadcast_to(m + jnp.log(l), (tq, A_HEAD_DIM))


def _attn_prompt(qkv, g, dil, batch, seq):
    width = qkv.shape[1]
    slabs = width // A_WIDTH
    n = seq // dil
    tq = A_STEPS
    view = qkv.reshape(batch, n, dil * width)

    def spec(role, prev):
        def imap(b, c, i):
            blk = jnp.maximum(i - 1, 0) if prev else i
            return (b, blk, c * slabs + g * 3 + role)
        return pl.BlockSpec((None, tq, A_WIDTH), imap)

    out_spec = pl.BlockSpec((None, tq, A_WIDTH), lambda b, c, i: (b, i, c))
    o, l = pl.pallas_call(
        _attn_prompt_kernel,
        out_shape=[jax.ShapeDtypeStruct((batch, n, dil * A_WIDTH), F32)] * 2,
        grid=(batch, dil, n // tq),
        in_specs=[spec(0, False), spec(1, True), spec(1, False), spec(2, True), spec(2, False)],
        out_specs=[out_spec, out_spec],
        compiler_params=_cparams("parallel", "parallel", "arbitrary"),
        name="attn_prompt_d%d" % dil,
    )(view, view, view, view, view)
    return o.reshape(batch * seq, A_WIDTH), l.reshape(batch * seq, A_WIDTH)


def _attn_sample_kernel(qkv_ref, c0_ref, c1_ref, c2_ref, o_ref):
    s_new = qkv_ref.shape[0]
    nrow = A_HEADS * s_new
    nbuf = A_STEPS
    ncol = 2 * nbuf
    row = lax.broadcasted_iota(jnp.int32, (nrow, ncol), 0)
    col = lax.broadcasted_iota(jnp.int32, (nrow, ncol), 1)
    sq = row & (s_new - 1)
    lane_w = lax.broadcasted_iota(jnp.int32, (nrow, A_WIDTH), 1)
    row_w = lax.broadcasted_iota(jnp.int32, (nrow, A_WIDTH), 0)
    own_head = (lane_w // A_HEAD_DIM) == (row_w // s_new)
    scale = A_HEAD_DIM ** -0.5
    group_w = 3 * A_WIDTH

    outs, lses = [], []
    for g, ((_, dil), cref) in enumerate(zip(A_GROUPS, (c0_ref, c1_ref, c2_ref))):
        shift = dil.bit_length() - 1
        q = qkv_ref[:, g * group_w: g * group_w + A_WIDTH]
        kn = qkv_ref[:, g * group_w + A_WIDTH: g * group_w + 2 * A_WIDTH]
        vn = qkv_ref[:, g * group_w + 2 * A_WIDTH: (g + 1) * group_w]
        qbd = jnp.where(own_head, jnp.concatenate([q] * A_HEADS, axis=0), 0.0).astype(BF16)
        knp = _pad_rows(kn, nbuf).astype(BF16)
        vnp = _pad_rows(vn, nbuf).astype(BF16)
        ncls = min(dil, s_new)
        row_cls = sq & (dil - 1)
        snew = col - nbuf
        valid = (((col < nbuf) & (col >= (sq >> shift)))
                 | ((snew >= 0) & (snew <= sq) & (((sq - snew) & (dil - 1)) == 0)))
        s_all = jnp.full((nrow, ncol), NEG_INF, F32)
        for c in range(ncls):
            kc = cref[:, c * 2 * A_WIDTH: c * 2 * A_WIDTH + A_WIDTH].astype(BF16)
            sc = _dot_nt(qbd, jnp.concatenate([kc, knp], axis=0)) * scale
            s_all = jnp.where(valid & (row_cls == c), sc, s_all)
        m = jnp.max(s_all, axis=-1, keepdims=True)
        p = jnp.exp(s_all - m)
        l = jnp.sum(p, axis=-1, keepdims=True)
        acc = jnp.zeros((nrow, A_WIDTH), F32)
        for c in range(ncls):
            vc = cref[:, c * 2 * A_WIDTH + A_WIDTH: (c + 1) * 2 * A_WIDTH].astype(BF16)
            pc = jnp.where(row_cls == c, p, 0.0).astype(BF16)
            acc = acc + _dot(pc, jnp.concatenate([vc, vnp], axis=0))
        outs.append(acc / l)
        lses.append(m + jnp.log(l))

    mx = jnp.maximum(jnp.maximum(lses[0], lses[1]), lses[2])
    es = [jnp.exp(x - mx) for x in lses]
    den = es[0] + es[1] + es[2]
    merged = (es[0] / den) * outs[0] + (es[1] / den) * outs[1] + (es[2] / den) * outs[2]
    lane_o = lax.broadcasted_iota(jnp.int32, (s_new, A_WIDTH), 1) // A_HEAD_DIM
    res = jnp.zeros((s_new, A_WIDTH), F32)
    for h in range(A_HEADS):
        res = res + jnp.where(lane_o == h, merged[h * s_new:(h + 1) * s_new, :], 0.0)
    o_ref[...] = res


def _attn_sample(qkv, caches, j, batch, s_new):
    width = qkv.shape[1]
    views = []
    specs = [pl.BlockSpec((s_new, width), lambda b: (b, 0))]
    for (win, dil), cache in zip(A_GROUPS, caches):
        n_past = cache.shape[2]
        assert n_past == dil * A_STEPS
        view = cache.reshape(cache.shape[0], batch, n_past // dil, dil * 2 * A_WIDTH)
        views.append(view)
        ncls = min(dil, s_new)
        specs.append(pl.BlockSpec((None, None, n_past // dil, ncls * 2 * A_WIDTH), lambda b: (j, b, 0, 0)))
    return pl.pallas_call(
        _attn_sample_kernel,
        out_shape=jax.ShapeDtypeStruct((batch * s_new, A_WIDTH), F32),
        grid=(batch,),
        in_specs=specs,
        out_specs=pl.BlockSpec((s_new, A_WIDTH), lambda b: (b, 0)),
        compiler_params=_cparams("parallel"),
        name="attn_sample",
    )(qkv, *views)


def _retention_kernel(q_ref, k_ref, v_ref, g_ref, s0_ref, dec_ref, inn_ref, tail_ref, cd_ref,
                      o_ref, sout_ref, s_scr):
    c = pl.program_id(1)
    lk = dec_ref.shape[2]

    @pl.when(c == 0)
    def _():
        s_scr[...] = s0_ref[...]

    for h in range(R_HEADS):
        ks = slice(h * R_DK, (h + 1) * R_DK)
        vs = slice(h * R_DV, (h + 1) * R_DV)
        q = q_ref[:, ks].astype(BF16)
        kf = _pad_rows(k_ref[:, ks] * (R_DK ** -0.5), lk)
        vb = _pad_rows(v_ref[:, vs], lk).astype(BF16)
        s = _dot_nt(q, kf.astype(BF16)) * dec_ref[h]
        state = s_scr[h]
        o = _dot(s.astype(BF16), vb) + _dot(q, state.astype(BF16)) * inn_ref[h]
        s_scr[h] = state * cd_ref[h] + _dot_tn((kf * tail_ref[h]).astype(BF16), vb)
        o_ref[:, vs] = (jax.nn.silu(g_ref[:, vs]) * _rms(o)).astype(o_ref.dtype)

    @pl.when(c == pl.num_programs(1) - 1)
    def _():
        sout_ref[...] = s_scr[...]


def _retention(proj, state, batch, seq):
    chunk = CHUNK if seq % CHUNK == 0 else seq
    lk = max(chunk, LANES)
    nc = seq // chunk
    nq, nv = R_HEADS * R_DK, R_HEADS * R_DV
    log_g = jnp.log1p(-jnp.exp2(-5.0 - jnp.arange(R_HEADS, dtype=F32)))
    i = jnp.arange(chunk, dtype=F32)
    diff = i[:, None] - i[None, :]
    decay = jnp.where(diff >= 0, jnp.exp(jnp.maximum(diff, 0.0)[None] * log_g[:, None, None]), 0.0)
    decay = jnp.pad(decay, ((0, 0), (0, 0), (0, lk - chunk)))
    inner = jnp.exp((i + 1.0)[None, :] * log_g[:, None])[:, :, None]
    tail = jnp.exp((chunk - 1.0 - i)[None, :] * log_g[:, None])
    tail = jnp.pad(tail, ((0, 0), (0, lk - chunk)))[:, :, None]
    cdec = jnp.exp(chunk * log_g)[:, None, None]
    out_dtype = BF16 if chunk % BF16_ROWS == 0 else F32
    whole = lambda b, c: (0, 0, 0)
    gated, new_state = pl.pallas_call(
        _retention_kernel,
        out_shape=[jax.ShapeDtypeStruct((batch * seq, nv), out_dtype),
                   jax.ShapeDtypeStruct(state.shape, F32)],
        grid=(batch, nc),
        in_specs=[
            pl.BlockSpec((chunk, nq), lambda b, c: (b * nc + c, 0)),
            pl.BlockSpec((chunk, nq), lambda b, c: (b * nc + c, 1)),
            pl.BlockSpec((chunk, nv), lambda b, c: (b * nc + c, 1)),
            pl.BlockSpec((chunk, nv), lambda b, c: (b * nc + c, 2)),
            pl.BlockSpec((None, R_HEADS, R_DK, R_DV), lambda b, c: (b, 0, 0, 0)),
            pl.BlockSpec((R_HEADS, chunk, lk), whole),
            pl.BlockSpec((R_HEADS, chunk, 1), whole),
            pl.BlockSpec((R_HEADS, lk, 1), whole),
            pl.BlockSpec((R_HEADS, 1, 1), whole),
        ],
        out_specs=[pl.BlockSpec((chunk, nv), lambda b, c: (b * nc + c, 0)),
                   pl.BlockSpec((None, R_HEADS, R_DK, R_DV), lambda b, c: (b, 0, 0, 0))],
        scratch_shapes=[pltpu.VMEM((R_HEADS, R_DK, R_DV), F32)],
        compiler_params=_cparams("parallel", "arbitrary"),
        name="retention",
    )(proj, proj, proj, proj, state, decay, inner, tail, cdec)
    return gated, new_state


def _mlstm_kernel(q_ref, k_ref, v_ref, og_ref, gt_ref, bias_ref, ng_ref, c0_ref, n0_ref, m0_ref,
                  o_ref, cout_ref, nout_ref, mout_ref, c_scr, n_scr, m_scr):
    c = pl.program_id(1)
    chunk = q_ref.shape[0]
    lk = max(chunk, LANES)

    @pl.when(c == 0)
    def _():
        c_scr[...] = c0_ref[...]
        n_scr[...] = n0_ref[...]
        m_scr[...] = m0_ref[...]

    pre = gt_ref[...] + bias_ref[...]
    log_f = -(jnp.maximum(-pre, 0.0) + jnp.log1p(jnp.exp(-jnp.abs(pre))))
    rowi = lax.broadcasted_iota(jnp.int32, pre.shape, 0)
    lane = lax.broadcasted_iota(jnp.int32, pre.shape, 1)
    cum = log_f
    d = 1
    while d < chunk:
        cum = cum + jnp.where(rowi >= d, pltpu.roll(cum, d, 0), 0.0)
        d *= 2
    cols = jnp.where(lane < M_HEADS, pre, cum)
    rows = _pad_rows(cols, lk).T
    t_idx = lax.broadcasted_iota(jnp.int32, (chunk, lk), 0)
    s_idx = lax.broadcasted_iota(jnp.int32, (chunk, lk), 1)
    causal = s_idx <= t_idx

    for h in range(M_HEADS):
        ks = slice(h * M_DK, (h + 1) * M_DK)
        vs = slice(h * M_DV, (h + 1) * M_DV)
        i_row = rows[h:h + 1, :]
        b_row = rows[M_HEADS + h:M_HEADS + h + 1, :]
        i_col = cols[:, h:h + 1]
        b_col = cols[:, M_HEADS + h:M_HEADS + h + 1]
        m_prev = m_scr[h:h + 1, 0:1]
        logw = jnp.where(causal, (b_col - b_row) + i_row, NEG_INF)
        inter = b_col + m_prev
        m_t = jnp.maximum(inter, jnp.max(logw, axis=-1, keepdims=True))
        w = jnp.exp(logw - m_t)
        a = jnp.exp(inter - m_t)

        qf = q_ref[:, ks]
        q = qf.astype(BF16)
        kf = k_ref[:, ks] * (M_DK ** -0.5)
        vb = _pad_rows(v_ref[:, vs], lk).astype(BF16)
        qk = _dot_nt(q, _pad_rows(kf, lk).astype(BF16)) * w
        cmat = c_scr[h]
        nrow = n_scr[h:h + 1, :]
        num = _dot(qk.astype(BF16), vb) + a * _dot(q, cmat.astype(BF16))
        den = jnp.sum(qk, axis=-1, keepdims=True) + a * jnp.sum(qf * nrow, axis=-1, keepdims=True)
        hid = num / jnp.maximum(jnp.abs(den), jnp.exp(-m_t))

        b_end = b_col[chunk - 1:chunk, :]
        m_end = m_t[chunk - 1:chunk, :]
        a_end = a[chunk - 1:chunk, :]
        kw = kf * jnp.exp(((b_end - b_col) + i_col) - m_end)
        c_scr[h] = a_end * cmat + _dot_tn(_pad_rows(kw, lk).astype(BF16), vb)
        n_scr[h:h + 1, :] = a_end * nrow + jnp.sum(kw, axis=0, keepdims=True)
        m_scr[h:h + 1, :] = jnp.broadcast_to(m_end, (1, m_scr.shape[1]))

        y = _rms(hid) * ng_ref[h:h + 1, :]
        o_ref[:, vs] = (jax.nn.sigmoid(og_ref[:, vs]) * y).astype(o_ref.dtype)

    @pl.when(c == pl.num_programs(1) - 1)
    def _():
        cout_ref[...] = c_scr[...]
        nout_ref[...] = n_scr[...]
        mout_ref[...] = m_scr[...]


def _mlstm(proj, gates, bias_row, norm_g, c0, n0, m0, batch, seq):
    chunk = CHUNK if seq % CHUNK == 0 else seq
    nc = seq // chunk
    nq, nv = M_HEADS * M_DK, M_HEADS * M_DV
    m0b = jnp.broadcast_to(m0[:, :, None], (batch, M_HEADS, LANES))
    out_dtype = BF16 if chunk % BF16_ROWS == 0 else F32
    st4 = lambda b, c: (b, 0, 0, 0)
    st3 = lambda b, c: (b, 0, 0)
    gated, c_new, n_new, m_new = pl.pallas_call(
        _mlstm_kernel,
        out_shape=[jax.ShapeDtypeStruct((batch * seq, nv), out_dtype),
                   jax.ShapeDtypeStruct(c0.shape, F32),
                   jax.ShapeDtypeStruct(n0.shape, F32),
                   jax.ShapeDtypeStruct(m0b.shape, F32)],
        grid=(batch, nc),
        in_specs=[
            pl.BlockSpec((chunk, nq), lambda b, c: (b * nc + c, 0)),
            pl.BlockSpec((chunk, nq), lambda b, c: (b * nc + c, 1)),
            pl.BlockSpec((chunk, nv), lambda b, c: (b * nc + c, 1)),
            pl.BlockSpec((chunk, nv), lambda b, c: (b * nc + c, 2)),
            pl.BlockSpec((chunk, LANES), lambda b, c: (b * nc + c, 0)),
            pl.BlockSpec((1, LANES), lambda b, c: (0, 0)),
            pl.BlockSpec((M_HEADS, M_DV), lambda b, c: (0, 0)),
            pl.BlockSpec((None, M_HEADS, M_DK, M_DV), st4),
            pl.BlockSpec((None, M_HEADS, M_DK), st3),
            pl.BlockSpec((None, M_HEADS, LANES), st3),
        ],
        out_specs=[pl.BlockSpec((chunk, nv), lambda b, c: (b * nc + c, 0)),
                   pl.BlockSpec((None, M_HEADS, M_DK, M_DV), st4),
                   pl.BlockSpec((None, M_HEADS, M_DK), st3),
                   pl.BlockSpec((None, M_HEADS, LANES), st3)],
        scratch_shapes=[pltpu.VMEM((M_HEADS, M_DK, M_DV), F32),
                        pltpu.VMEM((M_HEADS, M_DK), F32),
                        pltpu.VMEM((M_HEADS, LANES), F32)],
        compiler_params=_cparams("parallel", "arbitrary"),
        name="mlstm",
    )(proj, proj, proj, proj, gates, bias_row, norm_g, c0, n0, m0b)
    return gated, c_new, n_new, m_new[:, :, 0]


def _rope_tables_a(pos):
    inv = ROPE_THETA ** (-jnp.arange(ROPE_DIM // 2, dtype=F32) * (2.0 / ROPE_DIM))
    ang = pos.astype(F32)[:, None] * inv[None, :]
    cos, sin = jnp.cos(ang), jnp.sin(ang)
    half = ROPE_DIM // 2
    n = pos.shape[0]
    pad = A_HEAD_DIM - ROPE_DIM
    c = jnp.concatenate([cos, cos, jnp.ones((n, pad), F32)], axis=1)
    sp = jnp.concatenate([jnp.zeros((n, half), F32), sin, jnp.zeros((n, pad), F32)], axis=1)
    sm = jnp.concatenate([-sin, jnp.zeros((n, half + pad), F32)], axis=1)
    reps = LANES // A_HEAD_DIM
    return tuple(jnp.tile(t, (1, reps)) for t in (c, sp, sm))


def _rope_tables_b(pos):
    inv = R_THETA ** (-jnp.linspace(0.0, 1.0, R_DK // 2, dtype=F32))
    ang = pos.astype(F32)[:, None] * inv[None, :]
    return jnp.cos(ang), jnp.sin(ang)


def kernel(x_prompt, x_sample, cache_win128_kv, cache_win512_kv, cache_win2048_kv, state_ret,
           state_mlstm_c, state_mlstm_n, state_mlstm_m, state_ffn_conv, norm_gains,
           a_w_in, a_w_out, b_w_in, b_w_out, c_w_in, c_b_if, c_norm, c_w_out,
           f_w_up, f_conv_w, f_conv_b, f_w_down):
    bp, t, d = x_prompt.shape
    bs, s, _ = x_sample.shape
    depth = norm_gains.shape[0]
    caches = (cache_win128_kv, cache_win512_kv, cache_win2048_kv)
    pos_p = jnp.arange(t, dtype=jnp.int32)
    pos_s = jnp.tile(PAST_LEN + jnp.arange(s, dtype=jnp.int32), bs)
    tabs_a_p, tabs_a_s = _rope_tables_a(pos_p), _rope_tables_a(pos_s)
    tabs_b_p, tabs_b_s = _rope_tables_b(pos_p), _rope_tables_b(pos_s)
    tm = 512

    xp = x_prompt.reshape(bp * t, d)
    xs = x_sample.reshape(bs * s, d)
    win_p = [[] for _ in A_GROUPS]
    win_s = [[] for _ in A_GROUPS]
    ret_p, ret_s = [], []
    c_p, c_s, n_p, n_s, m_p, m_s = [], [], [], [], [], []
    conv_p, conv_s = [], []

    for layer in range(depth):
        kind, j = layer % N_MIXERS, layer // N_MIXERS
        g = norm_gains[layer]
        if kind == 0:
            w_in = a_w_in[j].astype(BF16)
            w_out = a_w_out[j].astype(BF16)
            qkv_p = _norm_proj(xp, g[0], w_in, mode="rope_a", tabs=tabs_a_p, tab_blocks=t // tm, tm=tm)
            qkv_s = _norm_proj(xs, g[0], w_in, mode="rope_a", tabs=tabs_a_s, tab_blocks=bs * s // tm, tm=tm)
            outs, lses = [], []
            for gi, (win, dil) in enumerate(A_GROUPS):
                o, l = _attn_prompt(qkv_p, gi, dil, bp, t)
                outs.append(o)
                lses.append(l)
                keep = min(win, t)
                lo = gi * 3 * A_WIDTH + A_WIDTH
                kv = qkv_p.reshape(bp, t, -1)[:, t - keep:, lo:lo + 2 * A_WIDTH]
                win_p[gi].append(kv.reshape(bp, keep, 2, A_HEADS, A_HEAD_DIM))
                kv = qkv_s.reshape(bs, s, -1)[:, :, lo:lo + 2 * A_WIDTH]
                win_s[gi].append(kv.reshape(bs, s, 2, A_HEADS, A_HEAD_DIM))
            xp = _post(outs + lses, w_out, xp, g[1], merge=True, tm=tm)
            att_s = _attn_sample(qkv_s, caches, j, bs, s)
            xs = _post([att_s], w_out, xs, g[1], merge=False, tm=tm)
        elif kind == 1:
            w_in = b_w_in[j].astype(BF16)
            w_out = b_w_out[j].astype(BF16)
            proj_p = _norm_proj(xp, g[0], w_in, mode="rope_b", tabs=tabs_b_p, tab_blocks=t // tm, tm=tm)
            proj_s = _norm_proj(xs, g[0], w_in, mode="rope_b", tabs=tabs_b_s, tab_blocks=bs * s // tm, tm=tm)
            gated_p, sp_new = _retention(proj_p, jnp.zeros((bp, R_HEADS, R_DK, R_DV), F32), bp, t)
            gated_s, ss_new = _retention(proj_s, state_ret[j], bs, s)
            ret_p.append(sp_new)
            ret_s.append(ss_new)
            xp = _post([gated_p], w_out, xp, g[1], merge=False, tm=tm)
            xs = _post([gated_s], w_out, xs, g[1], merge=False, tm=tm)
        else:
            n_main = 2 * M_HEADS * M_DK + 2 * M_HEADS * M_DV
            w_in = c_w_in[j][:, :n_main].astype(BF16)
            w_g = jnp.pad(c_w_in[j][:, n_main:], ((0, 0), (0, LANES - 2 * M_HEADS))).astype(BF16)
            w_out = c_w_out[j].astype(BF16)
            bias_row = jnp.pad(c_b_if[j].reshape(1, 2 * M_HEADS), ((0, 0), (0, LANES - 2 * M_HEADS)))
            proj_p, gates_p = _norm_proj(xp, g[0], w_in, mode="gates", wg=w_g, tm=tm)
            proj_s, gates_s = _norm_proj(xs, g[0], w_in, mode="gates", wg=w_g, tm=tm)
            zc = jnp.zeros((bp, M_HEADS, M_DK, M_DV), F32)
            zn = jnp.zeros((bp, M_HEADS, M_DK), F32)
            zm = jnp.zeros((bp, M_HEADS), F32)
            gated_p, cp, np_, mp = _mlstm(proj_p, gates_p, bias_row, c_norm[j], zc, zn, zm, bp, t)
            gated_s, cs, ns, ms = _mlstm(proj_s, gates_s, bias_row, c_norm[j], state_mlstm_c[j],
                                         state_mlstm_n[j], state_mlstm_m[j], bs, s)
            c_p.append(cp); c_s.append(cs)
            n_p.append(np_); n_s.append(ns)
            m_p.append(mp); m_s.append(ms)
            xp = _post([gated_p], w_out, xp, g[1], merge=False, tm=tm)
            xs = _post([gated_s], w_out, xs, g[1], merge=False, tm=tm)

        w_up = f_w_up[layer].astype(BF16)
        w_down = f_w_down[layer].astype(BF16)
        xp, tail = _ffn_prompt(xp, t, g[2], g[3], w_up, f_conv_w[layer], f_conv_b[layer], w_down, tm=tm)
        tail = tail.reshape(bp, t // tm, 8, D_FF)
        conv_p.append(tail[:, -1, 6:, :])
        buf = state_ffn_conv[layer]
        zrow = jnp.zeros((bs, s - 1, D_FF), F32)
        p1 = jnp.concatenate([buf[:, 1:2], zrow], axis=1).reshape(bs * s, D_FF)
        p2 = jnp.concatenate([buf, zrow[:, 1:]], axis=1).reshape(bs * s, D_FF)
        xs, a_s = _ffn_sample(xs, s, p1, p2, g[2], g[3], w_up, f_conv_w[layer], f_conv_b[layer], w_down, tm=tm)
        conv_s.append(a_s.reshape(bs, s, D_FF)[:, s - 2:, :])

    return (xp.reshape(bp, t, d), xs.reshape(bs, s, d),
            jnp.stack(win_p[0]), jnp.stack(win_s[0]),
            jnp.stack(win_p[1]), jnp.stack(win_s[1]),
            jnp.stack(win_p[2]), jnp.stack(win_s[2]),
            jnp.stack(ret_p), jnp.stack(ret_s),
            jnp.stack(c_p), jnp.stack(c_s),
            jnp.stack(n_p), jnp.stack(n_s),
            jnp.stack(m_p), jnp.stack(m_s),
            jnp.stack(conv_p), jnp.stack(conv_s))
```

```python
import functools

import jax
import jax.numpy as jnp
from jax import lax
from jax.experimental import pallas as pl
from jax.experimental.pallas import tpu as pltpu

F32 = jnp.float32
BF16 = jnp.bfloat16

D_MODEL = 1024
PAST_LEN = 2048
N_MIXERS = 3
A_GROUPS = ((128, 1), (512, 4), (2048, 16))
A_HEADS = 8
A_HEAD_DIM = 64
A_WIDTH = A_HEADS * A_HEAD_DIM
A_STEPS = 128
ROPE_THETA = 500000.0
ROPE_DIM = A_HEAD_DIM // 4
R_HEADS = 4
R_DK = 256
R_DV = 512
R_THETA = 10000.0
M_HEADS = 4
M_DK = 256
M_DV = 512
CHUNK = 128
D_FF = 4 * D_MODEL
EPS = 1e-6

LANES = 128
BF16_ROWS = 16
VMEM_LIMIT = 52 * 1024 * 1024

NT_DIMS = (((1,), (1,)), ((), ()))
TN_DIMS = (((0,), (0,)), ((), ()))
NEG_INF = float("-inf")


def _cparams(*sem):
    return pltpu.CompilerParams(dimension_semantics=sem, vmem_limit_bytes=VMEM_LIMIT)


def _rms(x):
    return x * lax.rsqrt(jnp.mean(x * x, axis=-1, keepdims=True) + EPS)


def _dot(a, b):
    return jnp.dot(a, b, preferred_element_type=F32)


def _dot_nt(a, b):
    return lax.dot_general(a, b, NT_DIMS, preferred_element_type=F32)


def _dot_tn(a, b):
    return lax.dot_general(a, b, TN_DIMS, preferred_element_type=F32)


def _pad_rows(x, rows):
    if x.shape[0] == rows:
        return x
    return jnp.concatenate([x, jnp.zeros((rows - x.shape[0], x.shape[1]), x.dtype)], axis=0)


def _norm_proj_kernel(*refs, mode):
    if mode == "rope_a":
        x_ref, g_ref, w_ref, c_ref, sp_ref, sm_ref, o_ref, h_scr = refs
    elif mode == "rope_b":
        x_ref, g_ref, w_ref, c_ref, s_ref, o_ref, h_scr = refs
    else:
        x_ref, g_ref, w_ref, wg_ref, o_ref, og_ref, h_scr = refs
    j = pl.program_id(1)

    @pl.when(j == 0)
    def _():
        h_scr[...] = (_rms(x_ref[...]) * g_ref[...]).astype(BF16)
        if mode == "gates":
            og_ref[...] = _dot(h_scr[...], wg_ref[...])

    acc = _dot(h_scr[...], w_ref[...])
    if mode == "rope_a":
        @pl.when(j % 3 != 2)
        def _():
            o_ref[...] = _rope_a(acc, c_ref, sp_ref, sm_ref)

        @pl.when(j % 3 == 2)
        def _():
            o_ref[...] = acc
    elif mode == "rope_b":
        @pl.when(j < 4)
        def _():
            c = c_ref[...]
            s = s_ref[...]
            outs = []
            for hh in range(acc.shape[1] // R_DK):
                x1 = acc[:, hh * R_DK: hh * R_DK + R_DK // 2]
                x2 = acc[:, hh * R_DK + R_DK // 2: (hh + 1) * R_DK]
                outs += [x1 * c - x2 * s, x2 * c + x1 * s]
            o_ref[...] = jnp.concatenate(outs, axis=1)

        @pl.when(j >= 4)
        def _():
            o_ref[...] = acc
    else:
        o_ref[...] = acc


def _norm_proj(x, gain, w, *, mode, tabs=(), tab_blocks=1, wg=None, tm=512, tn=512):
    n, d = x.shape
    m = w.shape[1]
    grid = (n // tm, m // tn)
    in_specs = [
        pl.BlockSpec((tm, d), lambda i, j: (i, 0)),
        pl.BlockSpec((1, d), lambda i, j: (0, 0)),
        pl.BlockSpec((d, tn), lambda i, j: (0, j)),
    ]
    args = [x, gain.reshape(1, d), w]
    out_shape = [jax.ShapeDtypeStruct((n, m), F32)]
    out_specs = [pl.BlockSpec((tm, tn), lambda i, j: (i, j))]
    if mode in ("rope_a", "rope_b"):
        for t in tabs:
            in_specs.append(pl.BlockSpec((tm, LANES), lambda i, j: (i % tab_blocks, 0)))
            args.append(t)
    else:
        in_specs.append(pl.BlockSpec((d, LANES), lambda i, j: (0, 0)))
        args.append(wg)
        out_shape.append(jax.ShapeDtypeStruct((n, LANES), F32))
        out_specs.append(pl.BlockSpec((tm, LANES), lambda i, j: (i, 0)))
    outs = pl.pallas_call(
        functools.partial(_norm_proj_kernel, mode=mode),
        out_shape=out_shape,
        grid=grid,
        in_specs=in_specs,
        out_specs=out_specs,
        scratch_shapes=[pltpu.VMEM((tm, d), BF16)],
        compiler_params=_cparams("parallel", "arbitrary"),
        name="norm_proj_" + mode,
    )(*args)
    return outs if mode == "gates" else outs[0]


def _rope_a(acc, c_ref, sp_ref, sm_ref):
    reps = acc.shape[1] // LANES
    c = jnp.concatenate([c_ref[...]] * reps, axis=1)
    sp = jnp.concatenate([sp_ref[...]] * reps, axis=1)
    sm = jnp.concatenate([sm_ref[...]] * reps, axis=1)
    half = ROPE_DIM // 2
    up = pltpu.roll(acc, half, 1)
    down = pltpu.roll(acc, acc.shape[1] - half, 1)
    return acc * c + down * sm + up * sp


def _proj_a_prompt_kernel(x_ref, g_ref, w_ref, c_ref, sp_ref, sm_ref, o0_ref, o1_ref, o2_ref, h_scr, acc_scr):
    j = pl.program_id(1)
    tm = x_ref.shape[0]
    chunks = A_WIDTH // LANES

    @pl.when(j == 0)
    def _():
        h_scr[...] = (_rms(x_ref[...]) * g_ref[...]).astype(BF16)

    acc = _dot(h_scr[...], w_ref[...])

    def stash(val):
        for k in range(chunks):
            acc_scr[k] = val[:, k * LANES:(k + 1) * LANES]

    @pl.when(j % 3 != 2)
    def _():
        stash(_rope_a(acc, c_ref, sp_ref, sm_ref))

    @pl.when(j % 3 == 2)
    def _():
        stash(acc)

    for g, o_ref in enumerate((o0_ref, o1_ref, o2_ref)):
        dil = o_ref.shape[0]
        rows = tm // dil

        @pl.when(j // 3 == g)
        def _(o_ref=o_ref, dil=dil, rows=rows):
            for c in range(dil):
                o_ref[c] = jnp.concatenate(
                    [acc_scr[k, pl.ds(c, rows, stride=dil), :] for k in range(chunks)], axis=1)


def _proj_a_prompt(x, gain, w, tabs, batch, seq, *, tm=1024):
    n, d = x.shape
    per_seq = seq // tm
    out_shape, out_specs = [], []
    for g, (_, dil) in enumerate(A_GROUPS):
        out_shape.append(jax.ShapeDtypeStruct((batch, dil, seq // dil, 3 * A_WIDTH), F32))
        out_specs.append(pl.BlockSpec(
            (None, dil, tm // dil, A_WIDTH),
            lambda i, j, g=g: (i // per_seq, 0, i % per_seq, jnp.clip(j - 3 * g, 0, 2))))
    in_specs = [
        pl.BlockSpec((tm, d), lambda i, j: (i, 0)),
        pl.BlockSpec((1, d), lambda i, j: (0, 0)),
        pl.BlockSpec((d, A_WIDTH), lambda i, j: (0, j)),
    ] + [pl.BlockSpec((tm, LANES), lambda i, j: (i % per_seq, 0)) for _ in tabs]
    return pl.pallas_call(
        _proj_a_prompt_kernel,
        out_shape=out_shape,
        grid=(n // tm, w.shape[1] // A_WIDTH),
        in_specs=in_specs,
        out_specs=out_specs,
        scratch_shapes=[pltpu.VMEM((tm, d), BF16), pltpu.VMEM((A_WIDTH // LANES, tm, LANES), F32)],
        compiler_params=_cparams("parallel", "arbitrary"),
        name="proj_a_prompt",
    )(x, gain.reshape(1, d), w, *tabs)


def _post_kernel(a_ref, w_ref, x_ref, g_ref, out_ref):
    y = _dot(a_ref[...].astype(BF16), w_ref[...])
    out_ref[...] = x_ref[...] + _rms(y) * g_ref[...]


def _post(a, w, x, gain, *, tm=512):
    n, d = x.shape
    k = w.shape[0]
    return pl.pallas_call(
        _post_kernel,
        out_shape=jax.ShapeDtypeStruct((n, d), F32),
        grid=(n // tm,),
        in_specs=[
            pl.BlockSpec((tm, k), lambda i: (i, 0)),
            pl.BlockSpec((k, d), lambda i: (0, 0)),
            pl.BlockSpec((tm, d), lambda i: (i, 0)),
            pl.BlockSpec((1, d), lambda i: (0, 0)),
        ],
        out_specs=pl.BlockSpec((tm, d), lambda i: (i, 0)),
        compiler_params=_cparams("parallel"),
        name="post",
    )(a, w, x, gain.reshape(1, d))


def _post_merge_kernel(o0, o1, o2, l0, l1, l2, w_ref, x_ref, g_ref, out_ref, scr):
    tm = x_ref.shape[0]
    chunks = A_WIDTH // LANES

    def token_order(ref, slot):
        dil = ref.shape[0]
        if dil == 1:
            return ref[0]
        rows = tm // dil
        for c in range(dil):
            v = ref[c]
            for k in range(chunks):
                scr[slot, k, pl.ds(c, rows, stride=dil), :] = v[:, k * LANES:(k + 1) * LANES]
        return jnp.concatenate([scr[slot, k] for k in range(chunks)], axis=1)

    la, lb, lc = token_order(l0, 0), token_order(l1, 0), token_order(l2, 1)
    mx = jnp.maximum(jnp.maximum(la, lb), lc)
    ea, eb, ec = jnp.exp(la - mx), jnp.exp(lb - mx), jnp.exp(lc - mx)
    den = ea + eb + ec
    a = (ea / den) * token_order(o0, 2) + (eb / den) * token_order(o1, 2) + (ec / den) * token_order(o2, 3)
    y = _dot(a.astype(BF16), w_ref[...])
    out_ref[...] = x_ref[...] + _rms(y) * g_ref[...]


def _post_merge(outs, lses, w, x, gain, seq, *, tm=512):
    n, d = x.shape
    k = w.shape[0]
    per_seq = seq // tm

    def cm_spec(a):
        dil = a.shape[1]
        return pl.BlockSpec((None, dil, tm // dil, k), lambda i: (i // per_seq, 0, i % per_seq, 0))

    return pl.pallas_call(
        _post_merge_kernel,
        out_shape=jax.ShapeDtypeStruct((n, d), F32),
        grid=(n // tm,),
        in_specs=[cm_spec(a) for a in outs + lses] + [
            pl.BlockSpec((k, d), lambda i: (0, 0)),
            pl.BlockSpec((tm, d), lambda i: (i, 0)),
            pl.BlockSpec((1, d), lambda i: (0, 0)),
        ],
        out_specs=pl.BlockSpec((tm, d), lambda i: (i, 0)),
        scratch_shapes=[pltpu.VMEM((4, k // LANES, tm, LANES), F32)],
        compiler_params=_cparams("parallel"),
        name="post_merge",
    )(*outs, *lses, w, x, gain.reshape(1, d))


def _ffn_tail(conv, b, wd_ref, acc_scr, x_ref, g3_ref, o_ref):
    j = pl.program_id(1)
    act = jax.nn.gelu(conv, approximate=True) * b
    acc_scr[...] += _dot(act.astype(BF16), wd_ref[...])

    @pl.when(j == pl.num_programs(1) - 1)
    def _():
        o_ref[...] = x_ref[...] + _rms(acc_scr[...]) * g3_ref[...]


def _ffn_prompt_kernel(x_ref, halo_ref, g2_ref, g3_ref, wa_ref, wb_ref, cw_ref, cb_ref, wd_ref,
                       o_ref, tail_ref, h_scr, acc_scr, *, tiles_per_seq):
    i = pl.program_id(0)
    j = pl.program_id(1)
    hr = halo_ref.shape[0]
    tm = x_ref.shape[0]

    @pl.when(j == 0)
    def _():
        keep = jnp.where(i % tiles_per_seq != 0, 1.0, 0.0)
        h_scr[:hr, :] = (_rms(halo_ref[...]) * g2_ref[...] * keep).astype(BF16)
        h_scr[hr:, :] = (_rms(x_ref[...]) * g2_ref[...]).astype(BF16)
        acc_scr[...] = jnp.zeros_like(acc_scr)

    a_ext = _dot(h_scr[...], wa_ref[...])
    b = _dot(h_scr[hr:, :], wb_ref[...])
    a = a_ext[hr:, :]
    prev1 = a_ext[hr - 1:hr, :]
    prev2 = a_ext[hr - 2:hr - 1, :]
    row = lax.broadcasted_iota(jnp.int32, a.shape, 0)
    r1 = jnp.where(row == 0, prev1, pltpu.roll(a, 1, 0))
    r2 = jnp.where(row == 0, prev2, jnp.where(row == 1, prev1, pltpu.roll(a, 2, 0)))
    cw = cw_ref[...]
    conv = cb_ref[...] + ((r2 * cw[0:1, :] + r1 * cw[1:2, :]) + a * cw[2:3, :])
    tail_ref[...] = a[tm - 8:, :]
    _ffn_tail(conv, b, wd_ref, acc_scr, x_ref, g3_ref, o_ref)


def _ffn_sample_kernel(x_ref, p1_ref, p2_ref, g2_ref, g3_ref, wa_ref, wb_ref, cw_ref, cb_ref, wd_ref,
                       o_ref, a_ref, h_scr, acc_scr, *, seq):
    j = pl.program_id(1)

    @pl.when(j == 0)
    def _():
        h_scr[...] = (_rms(x_ref[...]) * g2_ref[...]).astype(BF16)
        acc_scr[...] = jnp.zeros_like(acc_scr)

    a = _dot(h_scr[...], wa_ref[...])
    b = _dot(h_scr[...], wb_ref[...])
    pos = lax.broadcasted_iota(jnp.int32, a.shape, 0) & (seq - 1)
    r1 = jnp.where(pos == 0, p1_ref[...], pltpu.roll(a, 1, 0))
    r2 = jnp.where(pos < 2, p2_ref[...], pltpu.roll(a, 2, 0))
    cw = cw_ref[...]
    conv = cb_ref[...] + ((r2 * cw[0:1, :] + r1 * cw[1:2, :]) + a * cw[2:3, :])
    a_ref[...] = a
    _ffn_tail(conv, b, wd_ref, acc_scr, x_ref, g3_ref, o_ref)


def _ffn_common_specs(d, dff, tf):
    nf = dff // tf
    return [
        pl.BlockSpec((1, d), lambda i, j: (0, 0)),
        pl.BlockSpec((1, d), lambda i, j: (0, 0)),
        pl.BlockSpec((d, tf), lambda i, j: (0, j)),
        pl.BlockSpec((d, tf), lambda i, j: (0, j + nf)),
        pl.BlockSpec((3, tf), lambda i, j: (0, j)),
        pl.BlockSpec((1, tf), lambda i, j: (0, j)),
        pl.BlockSpec((tf, d), lambda i, j: (j, 0)),
    ]


def _ffn_prompt(x, seq_len, g2, g3, w_up, conv_w, conv_b, w_down, *, tm=512, tf=512):
    n, d = x.shape
    dff = w_down.shape[0]
    nt = n // tm
    hr = BF16_ROWS
    in_specs = [
        pl.BlockSpec((tm, d), lambda i, j: (i, 0)),
        pl.BlockSpec((hr, d), lambda i, j: (jnp.maximum(i * (tm // hr) - 1, 0), 0)),
    ] + _ffn_common_specs(d, dff, tf)
    y, tail = pl.pallas_call(
        functools.partial(_ffn_prompt_kernel, tiles_per_seq=seq_len // tm),
        out_shape=[jax.ShapeDtypeStruct((n, d), F32), jax.ShapeDtypeStruct((nt * 8, dff), F32)],
        grid=(nt, dff // tf),
        in_specs=in_specs,
        out_specs=[pl.BlockSpec((tm, d), lambda i, j: (i, 0)), pl.BlockSpec((8, tf), lambda i, j: (i, j))],
        scratch_shapes=[pltpu.VMEM((tm + hr, d), BF16), pltpu.VMEM((tm, d), F32)],
        compiler_params=_cparams("parallel", "arbitrary"),
        name="ffn_prompt",
    )(x, x, g2.reshape(1, d), g3.reshape(1, d), w_up, w_up, conv_w, conv_b.reshape(1, dff), w_down)
    return y, tail


def _ffn_sample(x, seq, p1, p2, g2, g3, w_up, conv_w, conv_b, w_down, *, tm=512, tf=512):
    n, d = x.shape
    dff = w_down.shape[0]
    in_specs = [
        pl.BlockSpec((tm, d), lambda i, j: (i, 0)),
        pl.BlockSpec((tm, tf), lambda i, j: (i, j)),
        pl.BlockSpec((tm, tf), lambda i, j: (i, j)),
    ] + _ffn_common_specs(d, dff, tf)
    y, a = pl.pallas_call(
        functools.partial(_ffn_sample_kernel, seq=seq),
        out_shape=[jax.ShapeDtypeStruct((n, d), F32), jax.ShapeDtypeStruct((n, dff), F32)],
        grid=(n // tm, dff // tf),
        in_specs=in_specs,
        out_specs=[pl.BlockSpec((tm, d), lambda i, j: (i, 0)), pl.BlockSpec((tm, tf), lambda i, j: (i, j))],
        scratch_shapes=[pltpu.VMEM((tm, d), BF16), pltpu.VMEM((tm, d), F32)],
        compiler_params=_cparams("parallel", "arbitrary"),
        name="ffn_sample",
    )(x, p1, p2, g2.reshape(1, d), g3.reshape(1, d), w_up, w_up, conv_w, conv_b.reshape(1, dff), w_down)
    return y, a


def _attn_prompt_kernel(q_ref, kp_ref, kc_ref, vp_ref, vc_ref, o_ref, l_ref):
    i = pl.program_id(2)
    tq = q_ref.shape[0]
    qi = lax.broadcasted_iota(jnp.int32, (tq, 2 * tq), 0) + tq
    ki = lax.broadcasted_iota(jnp.int32, (tq, 2 * tq), 1)
    dist = qi - ki
    first_key = jnp.where(i > 0, 0, tq)
    valid = (dist >= 0) & (dist <= A_STEPS) & (ki >= first_key)
    scale = A_HEAD_DIM ** -0.5
    for h in range(A_HEADS):
        hs = slice(h * A_HEAD_DIM, (h + 1) * A_HEAD_DIM)
        q = q_ref[:, hs].astype(BF16)
        k = jnp.concatenate([kp_ref[:, hs], kc_ref[:, hs]], axis=0).astype(BF16)
        v = jnp.concatenate([vp_ref[:, hs], vc_ref[:, hs]], axis=0).astype(BF16)
        s = jnp.where(valid, _dot_nt(q, k) * scale, NEG_INF)
        m = jnp.max(s, axis=-1, keepdims=True)
        p = jnp.exp(s - m)
        l = jnp.sum(p, axis=-1, keepdims=True)
        o_ref[:, hs] = _dot(p.astype(BF16), v) / l
        l_ref[:, hs] = jnp.broadcast_to(m + jnp.log(l), (tq, A_HEAD_DIM))


def _attn_prompt(qkv, batch, seq):
    dil, n = qkv.shape[1], qkv.shape[2]
    tq = A_STEPS

    def spec(role, prev):
        def imap(b, c, i):
            return (b, c, jnp.maximum(i - 1, 0) if prev else i, role)
        return pl.BlockSpec((None, None, tq, A_WIDTH), imap)

    out_spec = pl.BlockSpec((None, None, tq, A_WIDTH), lambda b, c, i: (b, c, i, 0))
    return pl.pallas_call(
        _attn_prompt_kernel,
        out_shape=[jax.ShapeDtypeStruct((batch, dil, n, A_WIDTH), F32)] * 2,
        grid=(batch, dil, n // tq),
        in_specs=[spec(0, False), spec(1, True), spec(1, False), spec(2, True), spec(2, False)],
        out_specs=[out_spec, out_spec],
        compiler_params=_cparams("parallel", "parallel", "arbitrary"),
        name="attn_prompt_d%d" % dil,
    )(qkv, qkv, qkv, qkv, qkv)


def _attn_sample_kernel(new_ref, c0_ref, c1_ref, c2_ref, o_ref):
    nrow = new_ref.shape[2]
    s_new = nrow // A_HEADS
    nkey = A_STEPS * A_HEADS
    scale = A_HEAD_DIM ** -0.5
    row = lax.broadcasted_iota(jnp.int32, (nrow, nkey), 0)
    col = lax.broadcasted_iota(jnp.int32, (nrow, nkey), 1)
    sq, hq = row >> 3, row & (A_HEADS - 1)
    same_head = (col & (A_HEADS - 1)) == hq
    mrow = col >> 3
    row_n = lax.broadcasted_iota(jnp.int32, (nrow, nrow), 0)
    col_n = lax.broadcasted_iota(jnp.int32, (nrow, nrow), 1)
    sq_n, s_key = row_n >> 3, col_n >> 3
    same_head_n = (col_n & (A_HEADS - 1)) == (row_n & (A_HEADS - 1))
    row_o = lax.broadcasted_iota(jnp.int32, (nrow, A_HEAD_DIM), 0) >> 3

    outs, lses = [], []
    for g, ((_, dil), cref) in enumerate(zip(A_GROUPS, (c0_ref, c1_ref, c2_ref))):
        shift = dil.bit_length() - 1
        ncls = cref.shape[1]
        q = new_ref[g, 0].astype(BF16)
        kn = new_ref[g, 1].astype(BF16)
        vn = new_ref[g, 2].astype(BF16)
        valid = same_head & (mrow >= (sq >> shift))
        s_buf = jnp.full((nrow, nkey), NEG_INF, F32)
        for c in range(ncls):
            kc = cref[:, c, 0].reshape(nkey, A_HEAD_DIM).astype(BF16)
            sc = _dot_nt(q, kc) * scale
            s_buf = jnp.where(valid & ((sq & (dil - 1)) == c), sc, s_buf)
        valid_n = same_head_n & (s_key <= sq_n) & (((sq_n - s_key) & (dil - 1)) == 0)
        s_own = jnp.where(valid_n, _dot_nt(q, kn) * scale, NEG_INF)
        m = jnp.maximum(jnp.max(s_buf, axis=-1, keepdims=True), jnp.max(s_own, axis=-1, keepdims=True))
        p_buf = jnp.exp(s_buf - m)
        p_own = jnp.exp(s_own - m)
        l = jnp.sum(p_buf, axis=-1, keepdims=True) + jnp.sum(p_own, axis=-1, keepdims=True)
        acc = _dot(p_own.astype(BF16), vn)
        for c in range(ncls):
            vc = cref[:, c, 1].reshape(nkey, A_HEAD_DIM).astype(BF16)
            pc = jnp.where((row_o & (dil - 1)) == c, 1.0, 0.0)
            acc = acc + pc * _dot(p_buf.astype(BF16), vc)
        outs.append(acc / l)
        lses.append(m + jnp.log(l))

    mx = jnp.maximum(jnp.maximum(lses[0], lses[1]), lses[2])
    es = [jnp.exp(x - mx) for x in lses]
    den = es[0] + es[1] + es[2]
    o_ref[...] = (es[0] / den) * outs[0] + (es[1] / den) * outs[1] + (es[2] / den) * outs[2]


def _attn_sample(qkv, caches, j, batch, s_new):
    new = qkv.reshape(batch, s_new, len(A_GROUPS), 3, A_HEADS, A_HEAD_DIM)
    new = new.transpose(0, 2, 3, 1, 4, 5).reshape(batch, len(A_GROUPS), 3, s_new * A_HEADS, A_HEAD_DIM)
    views = []
    specs = [pl.BlockSpec((None,) + new.shape[1:], lambda b: (b, 0, 0, 0, 0))]
    for (win, dil), cache in zip(A_GROUPS, caches):
        n_past = cache.shape[2]
        assert n_past == dil * A_STEPS
        view = cache.reshape(cache.shape[0], batch, A_STEPS, dil, 2, A_HEADS, A_HEAD_DIM)
        views.append(view)
        ncls = min(dil, s_new)
        specs.append(pl.BlockSpec((None, None, A_STEPS, ncls, 2, A_HEADS, A_HEAD_DIM),
                                  lambda b: (j, b, 0, 0, 0, 0, 0)))
    out = pl.pallas_call(
        _attn_sample_kernel,
        out_shape=jax.ShapeDtypeStruct((batch, s_new * A_HEADS, A_HEAD_DIM), F32),
        grid=(batch,),
        in_specs=specs,
        out_specs=pl.BlockSpec((None, s_new * A_HEADS, A_HEAD_DIM), lambda b: (b, 0, 0)),
        compiler_params=_cparams("parallel"),
        name="attn_sample",
    )(new, *views)
    return out.reshape(batch * s_new, A_WIDTH)


def _retention_kernel(q_ref, k_ref, v_ref, g_ref, s0_ref, dec_ref, inn_ref, tail_ref, cd_ref,
                      o_ref, sout_ref, s_scr):
    c = pl.program_id(1)
    lk = dec_ref.shape[2]

    @pl.when(c == 0)
    def _():
        s_scr[...] = s0_ref[...]

    for h in range(R_HEADS):
        ks = slice(h * R_DK, (h + 1) * R_DK)
        vs = slice(h * R_DV, (h + 1) * R_DV)
        q = q_ref[:, ks].astype(BF16)
        kf = _pad_rows(k_ref[:, ks] * (R_DK ** -0.5), lk)
        vb = _pad_rows(v_ref[:, vs], lk).astype(BF16)
        s = _dot_nt(q, kf.astype(BF16)) * dec_ref[h]
        state = s_scr[h]
        o = _dot(s.astype(BF16), vb) + _dot(q, state.astype(BF16)) * inn_ref[h]
        s_scr[h] = state * cd_ref[h] + _dot_tn((kf * tail_ref[h]).astype(BF16), vb)
        o_ref[:, vs] = (jax.nn.silu(g_ref[:, vs]) * _rms(o)).astype(o_ref.dtype)

    @pl.when(c == pl.num_programs(1) - 1)
    def _():
        sout_ref[...] = s_scr[...]


def _retention(proj, state, batch, seq):
    chunk = CHUNK if seq % CHUNK == 0 else seq
    lk = max(chunk, LANES)
    nc = seq // chunk
    nq, nv = R_HEADS * R_DK, R_HEADS * R_DV
    log_g = jnp.log1p(-jnp.exp2(-5.0 - jnp.arange(R_HEADS, dtype=F32)))
    i = jnp.arange(chunk, dtype=F32)
    diff = i[:, None] - i[None, :]
    decay = jnp.where(diff >= 0, jnp.exp(jnp.maximum(diff, 0.0)[None] * log_g[:, None, None]), 0.0)
    decay = jnp.pad(decay, ((0, 0), (0, 0), (0, lk - chunk)))
    inner = jnp.exp((i + 1.0)[None, :] * log_g[:, None])[:, :, None]
    tail = jnp.exp((chunk - 1.0 - i)[None, :] * log_g[:, None])
    tail = jnp.pad(tail, ((0, 0), (0, lk - chunk)))[:, :, None]
    cdec = jnp.exp(chunk * log_g)[:, None, None]
    out_dtype = BF16 if chunk % BF16_ROWS == 0 else F32
    whole = lambda b, c: (0, 0, 0)
    gated, new_state = pl.pallas_call(
        _retention_kernel,
        out_shape=[jax.ShapeDtypeStruct((batch * seq, nv), out_dtype),
                   jax.ShapeDtypeStruct(state.shape, F32)],
        grid=(batch, nc),
        in_specs=[
            pl.BlockSpec((chunk, nq), lambda b, c: (b * nc + c, 0)),
            pl.BlockSpec((chunk, nq), lambda b, c: (b * nc + c, 1)),
            pl.BlockSpec((chunk, nv), lambda b, c: (b * nc + c, 1)),
            pl.BlockSpec((chunk, nv), lambda b, c: (b * nc + c, 2)),
            pl.BlockSpec((None, R_HEADS, R_DK, R_DV), lambda b, c: (b, 0, 0, 0)),
            pl.BlockSpec((R_HEADS, chunk, lk), whole),
            pl.BlockSpec((R_HEADS, chunk, 1), whole),
            pl.BlockSpec((R_HEADS, lk, 1), whole),
            pl.BlockSpec((R_HEADS, 1, 1), whole),
        ],
        out_specs=[pl.BlockSpec((chunk, nv), lambda b, c: (b * nc + c, 0)),
                   pl.BlockSpec((None, R_HEADS, R_DK, R_DV), lambda b, c: (b, 0, 0, 0))],
        scratch_shapes=[pltpu.VMEM((R_HEADS, R_DK, R_DV), F32)],
        compiler_params=_cparams("parallel", "arbitrary"),
        name="retention",
    )(proj, proj, proj, proj, state, decay, inner, tail, cdec)
    return gated, new_state


def _mlstm_kernel(q_ref, k_ref, v_ref, og_ref, gt_ref, bias_ref, ng_ref, c0_ref, n0_ref, m0_ref,
                  o_ref, cout_ref, nout_ref, mout_ref, c_scr, n_scr, m_scr):
    c = pl.program_id(1)
    chunk = q_ref.shape[0]
    lk = max(chunk, LANES)

    @pl.when(c == 0)
    def _():
        c_scr[...] = c0_ref[...]
        n_scr[...] = n0_ref[...]
        m_scr[...] = m0_ref[...]

    pre = gt_ref[...] + bias_ref[...]
    log_f = -(jnp.maximum(-pre, 0.0) + jnp.log1p(jnp.exp(-jnp.abs(pre))))
    rowi = lax.broadcasted_iota(jnp.int32, pre.shape, 0)
    lane = lax.broadcasted_iota(jnp.int32, pre.shape, 1)
    cum = log_f
    d = 1
    while d < chunk:
        cum = cum + jnp.where(rowi >= d, pltpu.roll(cum, d, 0), 0.0)
        d *= 2
    cols = jnp.where(lane < M_HEADS, pre, cum)
    rows = _pad_rows(cols, lk).T
    t_idx = lax.broadcasted_iota(jnp.int32, (chunk, lk), 0)
    s_idx = lax.broadcasted_iota(jnp.int32, (chunk, lk), 1)
    causal = s_idx <= t_idx

    for h in range(M_HEADS):
        ks = slice(h * M_DK, (h + 1) * M_DK)
        vs = slice(h * M_DV, (h + 1) * M_DV)
        i_row = rows[h:h + 1, :]
        b_row = rows[M_HEADS + h:M_HEADS + h + 1, :]
        i_col = cols[:, h:h + 1]
        b_col = cols[:, M_HEADS + h:M_HEADS + h + 1]
        m_prev = m_scr[h:h + 1, 0:1]
        logw = jnp.where(causal, (b_col - b_row) + i_row, NEG_INF)
        inter = b_col + m_prev
        m_t = jnp.maximum(inter, jnp.max(logw, axis=-1, keepdims=True))
        w = jnp.exp(logw - m_t)
        a = jnp.exp(inter - m_t)

        qf = q_ref[:, ks]
        q = qf.astype(BF16)
        kf = k_ref[:, ks] * (M_DK ** -0.5)
        vb = _pad_rows(v_ref[:, vs], lk).astype(BF16)
        qk = _dot_nt(q, _pad_rows(kf, lk).astype(BF16)) * w
        cmat = c_scr[h]
        nrow = n_scr[h:h + 1, :]
        num = _dot(qk.astype(BF16), vb) + a * _dot(q, cmat.astype(BF16))
        den = jnp.sum(qk, axis=-1, keepdims=True) + a * jnp.sum(qf * nrow, axis=-1, keepdims=True)
        hid = num / jnp.maximum(jnp.abs(den), jnp.exp(-m_t))

        b_end = b_col[chunk - 1:chunk, :]
        m_end = m_t[chunk - 1:chunk, :]
        a_end = a[chunk - 1:chunk, :]
        kw = kf * jnp.exp(((b_end - b_col) + i_col) - m_end)
        c_scr[h] = a_end * cmat + _dot_tn(_pad_rows(kw, lk).astype(BF16), vb)
        n_scr[h:h + 1, :] = a_end * nrow + jnp.sum(kw, axis=0, keepdims=True)
        m_scr[h:h + 1, :] = jnp.broadcast_to(m_end, (1, m_scr.shape[1]))

        y = _rms(hid) * ng_ref[h:h + 1, :]
        o_ref[:, vs] = (jax.nn.sigmoid(og_ref[:, vs]) * y).astype(o_ref.dtype)

    @pl.when(c == pl.num_programs(1) - 1)
    def _():
        cout_ref[...] = c_scr[...]
        nout_ref[...] = n_scr[...]
        mout_ref[...] = m_scr[...]


def _mlstm(proj, gates, bias_row, norm_g, c0, n0, m0, batch, seq):
    chunk = CHUNK if seq % CHUNK == 0 else seq
    nc = seq // chunk
    nq, nv = M_HEADS * M_DK, M_HEADS * M_DV
    m0b = jnp.broadcast_to(m0[:, :, None], (batch, M_HEADS, LANES))
    out_dtype = BF16 if chunk % BF16_ROWS == 0 else F32
    st4 = lambda b, c: (b, 0, 0, 0)
    st3 = lambda b, c: (b, 0, 0)
    gated, c_new, n_new, m_new = pl.pallas_call(
        _mlstm_kernel,
        out_shape=[jax.ShapeDtypeStruct((batch * seq, nv), out_dtype),
                   jax.ShapeDtypeStruct(c0.shape, F32),
                   jax.ShapeDtypeStruct(n0.shape, F32),
                   jax.ShapeDtypeStruct(m0b.shape, F32)],
        grid=(batch, nc),
        in_specs=[
            pl.BlockSpec((chunk, nq), lambda b, c: (b * nc + c, 0)),
            pl.BlockSpec((chunk, nq), lambda b, c: (b * nc + c, 1)),
            pl.BlockSpec((chunk, nv), lambda b, c: (b * nc + c, 1)),
            pl.BlockSpec((chunk, nv), lambda b, c: (b * nc + c, 2)),
            pl.BlockSpec((chunk, LANES), lambda b, c: (b * nc + c, 0)),
            pl.BlockSpec((1, LANES), lambda b, c: (0, 0)),
            pl.BlockSpec((M_HEADS, M_DV), lambda b, c: (0, 0)),
            pl.BlockSpec((None, M_HEADS, M_DK, M_DV), st4),
            pl.BlockSpec((None, M_HEADS, M_DK), st3),
            pl.BlockSpec((None, M_HEADS, LANES), st3),
        ],
        out_specs=[pl.BlockSpec((chunk, nv), lambda b, c: (b * nc + c, 0)),
                   pl.BlockSpec((None, M_HEADS, M_DK, M_DV), st4),
                   pl.BlockSpec((None, M_HEADS, M_DK), st3),
                   pl.BlockSpec((None, M_HEADS, LANES), st3)],
        scratch_shapes=[pltpu.VMEM((M_HEADS, M_DK, M_DV), F32),
                        pltpu.VMEM((M_HEADS, M_DK), F32),
                        pltpu.VMEM((M_HEADS, LANES), F32)],
        compiler_params=_cparams("parallel", "arbitrary"),
        name="mlstm",
    )(proj, proj, proj, proj, gates, bias_row, norm_g, c0, n0, m0b)
    return gated, c_new, n_new, m_new[:, :, 0]


def _rope_tables_a(pos):
    inv = ROPE_THETA ** (-jnp.arange(ROPE_DIM // 2, dtype=F32) * (2.0 / ROPE_DIM))
    ang = pos.astype(F32)[:, None] * inv[None, :]
    cos, sin = jnp.cos(ang), jnp.sin(ang)
    half = ROPE_DIM // 2
    n = pos.shape[0]
    pad = A_HEAD_DIM - ROPE_DIM
    c = jnp.concatenate([cos, cos, jnp.ones((n, pad), F32)], axis=1)
    sp = jnp.concatenate([jnp.zeros((n, half), F32), sin, jnp.zeros((n, pad), F32)], axis=1)
    sm = jnp.concatenate([-sin, jnp.zeros((n, half + pad), F32)], axis=1)
    reps = LANES // A_HEAD_DIM
    return tuple(jnp.tile(t, (1, reps)) for t in (c, sp, sm))


def _rope_tables_b(pos):
    inv = R_THETA ** (-jnp.linspace(0.0, 1.0, R_DK // 2, dtype=F32))
    ang = pos.astype(F32)[:, None] * inv[None, :]
    return jnp.cos(ang), jnp.sin(ang)


def kernel(x_prompt, x_sample, cache_win128_kv, cache_win512_kv, cache_win2048_kv, state_ret,
           state_mlstm_c, state_mlstm_n, state_mlstm_m, state_ffn_conv, norm_gains,
           a_w_in, a_w_out, b_w_in, b_w_out, c_w_in, c_b_if, c_norm, c_w_out,
           f_w_up, f_conv_w, f_conv_b, f_w_down):
    bp, t, d = x_prompt.shape
    bs, s, _ = x_sample.shape
    depth = norm_gains.shape[0]
    caches = (cache_win128_kv, cache_win512_kv, cache_win2048_kv)
    pos_p = jnp.arange(t, dtype=jnp.int32)
    pos_s = jnp.tile(PAST_LEN + jnp.arange(s, dtype=jnp.int32), bs)
    tabs_a_p, tabs_a_s = _rope_tables_a(pos_p), _rope_tables_a(pos_s)
    tabs_b_p, tabs_b_s = _rope_tables_b(pos_p), _rope_tables_b(pos_s)
    tm = 512
    tm_p = 1024

    xp = x_prompt.reshape(bp * t, d)
    xs = x_sample.reshape(bs * s, d)
    win_p = [[] for _ in A_GROUPS]
    win_s = [[] for _ in A_GROUPS]
    ret_p, ret_s = [], []
    c_p, c_s, n_p, n_s, m_p, m_s = [], [], [], [], [], []
    conv_p, conv_s = [], []

    for layer in range(depth):
        kind, j = layer % N_MIXERS, layer // N_MIXERS
        g = norm_gains[layer]
        if kind == 0:
            w_in = a_w_in[j].astype(BF16)
            w_out = a_w_out[j].astype(BF16)
            qkv_p = _proj_a_prompt(xp, g[0], w_in, tabs_a_p, bp, t, tm=tm_p)
            qkv_s = _norm_proj(xs, g[0], w_in, mode="rope_a", tabs=tabs_a_s, tab_blocks=bs * s // tm, tm=tm)
            outs, lses = [], []
            for gi, (win, dil) in enumerate(A_GROUPS):
                o, l = _attn_prompt(qkv_p[gi], bp, t)
                outs.append(o)
                lses.append(l)
                keep = min(win, t)
                kv = qkv_p[gi][:, :, (t - keep) // dil:, A_WIDTH:].transpose(0, 2, 1, 3)
                win_p[gi].append(kv.reshape(bp, keep, 2, A_HEADS, A_HEAD_DIM))
                lo = gi * 3 * A_WIDTH + A_WIDTH
                kv = qkv_s.reshape(bs, s, -1)[:, :, lo:lo + 2 * A_WIDTH]
                win_s[gi].append(kv.reshape(bs, s, 2, A_HEADS, A_HEAD_DIM))
            xp = _post_merge(outs, lses, w_out, xp, g[1], t, tm=tm)
            att_s = _attn_sample(qkv_s, caches, j, bs, s)
            xs = _post(att_s, w_out, xs, g[1], tm=tm)
        elif kind == 1:
            w_in = b_w_in[j].astype(BF16)
            w_out = b_w_out[j].astype(BF16)
            proj_p = _norm_proj(xp, g[0], w_in, mode="rope_b", tabs=tabs_b_p, tab_blocks=t // tm_p, tm=tm_p)
            proj_s = _norm_proj(xs, g[0], w_in, mode="rope_b", tabs=tabs_b_s, tab_blocks=bs * s // tm, tm=tm)
            gated_p, sp_new = _retention(proj_p, jnp.zeros((bp, R_HEADS, R_DK, R_DV), F32), bp, t)
            gated_s, ss_new = _retention(proj_s, state_ret[j], bs, s)
            ret_p.append(sp_new)
            ret_s.append(ss_new)
            xp = _post(gated_p, w_out, xp, g[1], tm=tm)
            xs = _post(gated_s, w_out, xs, g[1], tm=tm)
        else:
            n_main = 2 * M_HEADS * M_DK + 2 * M_HEADS * M_DV
            w_in = c_w_in[j][:, :n_main].astype(BF16)
            w_g = jnp.pad(c_w_in[j][:, n_main:], ((0, 0), (0, LANES - 2 * M_HEADS))).astype(BF16)
            w_out = c_w_out[j].astype(BF16)
            bias_row = jnp.pad(c_b_if[j].reshape(1, 2 * M_HEADS), ((0, 0), (0, LANES - 2 * M_HEADS)))
            proj_p, gates_p = _norm_proj(xp, g[0], w_in, mode="gates", wg=w_g, tm=tm_p)
            proj_s, gates_s = _norm_proj(xs, g[0], w_in, mode="gates", wg=w_g, tm=tm)
            zc = jnp.zeros((bp, M_HEADS, M_DK, M_DV), F32)
            zn = jnp.zeros((bp, M_HEADS, M_DK), F32)
            zm = jnp.zeros((bp, M_HEADS), F32)
            gated_p, cp, np_, mp = _mlstm(proj_p, gates_p, bias_row, c_norm[j], zc, zn, zm, bp, t)
            gated_s, cs, ns, ms = _mlstm(proj_s, gates_s, bias_row, c_norm[j], state_mlstm_c[j],
                                         state_mlstm_n[j], state_mlstm_m[j], bs, s)
            c_p.append(cp); c_s.append(cs)
            n_p.append(np_); n_s.append(ns)
            m_p.append(mp); m_s.append(ms)
            xp = _post(gated_p, w_out, xp, g[1], tm=tm)
            xs = _post(gated_s, w_out, xs, g[1], tm=tm)

        w_up = f_w_up[layer].astype(BF16)
        w_down = f_w_down[layer].astype(BF16)
        xp, tail = _ffn_prompt(xp, t, g[2], g[3], w_up, f_conv_w[layer], f_conv_b[layer], w_down, tm=tm_p)
        tail = tail.reshape(bp, t // tm_p, 8, D_FF)
        conv_p.append(tail[:, -1, 6:, :])
        buf = state_ffn_conv[layer]
        zrow = jnp.zeros((bs, s - 1, D_FF), F32)
        p1 = jnp.concatenate([buf[:, 1:2], zrow], axis=1).reshape(bs * s, D_FF)
        p2 = jnp.concatenate([buf, zrow[:, 1:]], axis=1).reshape(bs * s, D_FF)
        xs, a_s = _ffn_sample(xs, s, p1, p2, g[2], g[3], w_up, f_conv_w[layer], f_conv_b[layer], w_down, tm=tm)
        conv_s.append(a_s.reshape(bs, s, D_FF)[:, s - 2:, :])

    return (xp.reshape(bp, t, d), xs.reshape(bs, s, d),
            jnp.stack(win_p[0]), jnp.stack(win_s[0]),
            jnp.stack(win_p[1]), jnp.stack(win_s[1]),
            jnp.stack(win_p[2]), jnp.stack(win_s[2]),
            jnp.stack(ret_p), jnp.stack(ret_s),
            jnp.stack(c_p), jnp.stack(c_s),
            jnp.stack(n_p), jnp.stack(n_s),
            jnp.stack(m_p), jnp.stack(m_s),
            jnp.stack(conv_p), jnp.stack(conv_s))
```

```python
import functools

import jax
import jax.numpy as jnp
from jax import lax
from jax.experimental import pallas as pl
from jax.experimental.pallas import tpu as pltpu

F32 = jnp.float32
BF16 = jnp.bfloat16

D_MODEL = 1024
PAST_LEN = 2048
N_MIXERS = 3
A_GROUPS = ((128, 1), (512, 4), (2048, 16))
A_HEADS = 8
A_HEAD_DIM = 64
A_WIDTH = A_HEADS * A_HEAD_DIM
A_STEPS = 128
ROPE_THETA = 500000.0
ROPE_DIM = A_HEAD_DIM // 4
R_HEADS = 4
R_DK = 256
R_DV = 512
R_THETA = 10000.0
M_HEADS = 4
M_DK = 256
M_DV = 512
CHUNK = 128
D_FF = 4 * D_MODEL
EPS = 1e-6

LANES = 128
BF16_ROWS = 16
VMEM_LIMIT = 52 * 1024 * 1024

NT_DIMS = (((1,), (1,)), ((), ()))
TN_DIMS = (((0,), (0,)), ((), ()))
NEG_INF = float("-inf")


def _cparams(*sem):
    return pltpu.CompilerParams(dimension_semantics=sem, vmem_limit_bytes=VMEM_LIMIT)


def _rms(x):
    return x * lax.rsqrt(jnp.mean(x * x, axis=-1, keepdims=True) + EPS)


def _dot(a, b):
    return jnp.dot(a, b, preferred_element_type=F32)


def _dot_nt(a, b):
    return lax.dot_general(a, b, NT_DIMS, preferred_element_type=F32)


def _dot_tn(a, b):
    return lax.dot_general(a, b, TN_DIMS, preferred_element_type=F32)


def _pad_rows(x, rows):
    if x.shape[0] == rows:
        return x
    return jnp.concatenate([x, jnp.zeros((rows - x.shape[0], x.shape[1]), x.dtype)], axis=0)


def _norm_proj_kernel(*refs, mode):
    if mode == "rope_a":
        x_ref, g_ref, w_ref, c_ref, sp_ref, sm_ref, o_ref, h_scr = refs
    elif mode == "rope_b":
        x_ref, g_ref, w_ref, c_ref, s_ref, o_ref, h_scr = refs
    else:
        x_ref, g_ref, w_ref, wg_ref, o_ref, og_ref, h_scr = refs
    j = pl.program_id(1)

    @pl.when(j == 0)
    def _():
        h_scr[...] = (_rms(x_ref[...]) * g_ref[...]).astype(BF16)
        if mode == "gates":
            og_ref[...] = _dot(h_scr[...], wg_ref[...])

    acc = _dot(h_scr[...], w_ref[...])
    if mode == "rope_a":
        @pl.when(j % 3 != 2)
        def _():
            o_ref[...] = _rope_a(acc, c_ref, sp_ref, sm_ref)

        @pl.when(j % 3 == 2)
        def _():
            o_ref[...] = acc
    elif mode == "rope_b":
        @pl.when(j < 4)
        def _():
            c = c_ref[...]
            s = s_ref[...]
            outs = []
            for hh in range(acc.shape[1] // R_DK):
                x1 = acc[:, hh * R_DK: hh * R_DK + R_DK // 2]
                x2 = acc[:, hh * R_DK + R_DK // 2: (hh + 1) * R_DK]
                outs += [x1 * c - x2 * s, x2 * c + x1 * s]
            o_ref[...] = jnp.concatenate(outs, axis=1)

        @pl.when(j >= 4)
        def _():
            o_ref[...] = acc
    else:
        o_ref[...] = acc


def _norm_proj(x, gain, w, *, mode, tabs=(), tab_blocks=1, wg=None, tm=512, tn=512):
    n, d = x.shape
    m = w.shape[1]
    grid = (n // tm, m // tn)
    in_specs = [
        pl.BlockSpec((tm, d), lambda i, j: (i, 0)),
        pl.BlockSpec((1, d), lambda i, j: (0, 0)),
        pl.BlockSpec((d, tn), lambda i, j: (0, j)),
    ]
    args = [x, gain.reshape(1, d), w]
    out_shape = [jax.ShapeDtypeStruct((n, m), F32)]
    out_specs = [pl.BlockSpec((tm, tn), lambda i, j: (i, j))]
    if mode in ("rope_a", "rope_b"):
        for t in tabs:
            in_specs.append(pl.BlockSpec((tm, LANES), lambda i, j: (i % tab_blocks, 0)))
            args.append(t)
    else:
        in_specs.append(pl.BlockSpec((d, LANES), lambda i, j: (0, 0)))
        args.append(wg)
        out_shape.append(jax.ShapeDtypeStruct((n, LANES), F32))
        out_specs.append(pl.BlockSpec((tm, LANES), lambda i, j: (i, 0)))
    outs = pl.pallas_call(
        functools.partial(_norm_proj_kernel, mode=mode),
        out_shape=out_shape,
        grid=grid,
        in_specs=in_specs,
        out_specs=out_specs,
        scratch_shapes=[pltpu.VMEM((tm, d), BF16)],
        compiler_params=_cparams("parallel", "arbitrary"),
        name="norm_proj_" + mode,
    )(*args)
    return outs if mode == "gates" else outs[0]


def _rope_a(acc, c_ref, sp_ref, sm_ref):
    reps = acc.shape[1] // LANES
    c = jnp.concatenate([c_ref[...]] * reps, axis=1)
    sp = jnp.concatenate([sp_ref[...]] * reps, axis=1)
    sm = jnp.concatenate([sm_ref[...]] * reps, axis=1)
    half = ROPE_DIM // 2
    up = pltpu.roll(acc, half, 1)
    down = pltpu.roll(acc, acc.shape[1] - half, 1)
    return acc * c + down * sm + up * sp


def _proj_a_prompt_kernel(x_ref, g_ref, w_ref, c_ref, sp_ref, sm_ref, o0_ref, o1_ref, o2_ref, h_scr, acc_scr):
    j = pl.program_id(1)
    tm = x_ref.shape[0]
    chunks = A_WIDTH // LANES

    @pl.when(j == 0)
    def _():
        h_scr[...] = (_rms(x_ref[...]) * g_ref[...]).astype(BF16)

    acc = _dot(h_scr[...], w_ref[...])

    def stash(val):
        for k in range(chunks):
            acc_scr[k] = val[:, k * LANES:(k + 1) * LANES]

    @pl.when(j % 3 != 2)
    def _():
        stash(_rope_a(acc, c_ref, sp_ref, sm_ref))

    @pl.when(j % 3 == 2)
    def _():
        stash(acc)

    for g, o_ref in enumerate((o0_ref, o1_ref, o2_ref)):
        dil = o_ref.shape[0]
        rows = tm // dil

        @pl.when(j // 3 == g)
        def _(o_ref=o_ref, dil=dil, rows=rows):
            for c in range(dil):
                o_ref[c] = jnp.concatenate(
                    [acc_scr[k, pl.ds(c, rows, stride=dil), :] for k in range(chunks)], axis=1)


def _proj_a_prompt(x, gain, w, tabs, batch, seq, *, tm=1024):
    n, d = x.shape
    per_seq = seq // tm
    out_shape, out_specs = [], []
    for g, (_, dil) in enumerate(A_GROUPS):
        out_shape.append(jax.ShapeDtypeStruct((batch, dil, seq // dil, 3 * A_WIDTH), F32))
        out_specs.append(pl.BlockSpec(
            (None, dil, tm // dil, A_WIDTH),
            lambda i, j, g=g: (i // per_seq, 0, i % per_seq, jnp.clip(j - 3 * g, 0, 2))))
    in_specs = [
        pl.BlockSpec((tm, d), lambda i, j: (i, 0)),
        pl.BlockSpec((1, d), lambda i, j: (0, 0)),
        pl.BlockSpec((d, A_WIDTH), lambda i, j: (0, j)),
    ] + [pl.BlockSpec((tm, LANES), lambda i, j: (i % per_seq, 0)) for _ in tabs]
    return pl.pallas_call(
        _proj_a_prompt_kernel,
        out_shape=out_shape,
        grid=(n // tm, w.shape[1] // A_WIDTH),
        in_specs=in_specs,
        out_specs=out_specs,
        scratch_shapes=[pltpu.VMEM((tm, d), BF16), pltpu.VMEM((A_WIDTH // LANES, tm, LANES), F32)],
        compiler_params=_cparams("parallel", "arbitrary"),
        name="proj_a_prompt",
    )(x, gain.reshape(1, d), w, *tabs)


def _post_kernel(a_ref, w_ref, x_ref, g_ref, out_ref):
    y = _dot(a_ref[...].astype(BF16), w_ref[...])
    out_ref[...] = x_ref[...] + _rms(y) * g_ref[...]


def _post(a, w, x, gain, *, tm=512):
    n, d = x.shape
    k = w.shape[0]
    return pl.pallas_call(
        _post_kernel,
        out_shape=jax.ShapeDtypeStruct((n, d), F32),
        grid=(n // tm,),
        in_specs=[
            pl.BlockSpec((tm, k), lambda i: (i, 0)),
            pl.BlockSpec((k, d), lambda i: (0, 0)),
            pl.BlockSpec((tm, d), lambda i: (i, 0)),
            pl.BlockSpec((1, d), lambda i: (0, 0)),
        ],
        out_specs=pl.BlockSpec((tm, d), lambda i: (i, 0)),
        compiler_params=_cparams("parallel"),
        name="post",
    )(a, w, x, gain.reshape(1, d))


def _post_merge_kernel(o0, o1, o2, l0, l1, l2, w_ref, x_ref, g_ref, out_ref, scr):
    tm = x_ref.shape[0]
    chunks = A_WIDTH // LANES

    def token_order(ref, slot):
        dil = ref.shape[0]
        if dil == 1:
            return ref[0]
        rows = tm // dil
        for c in range(dil):
            v = ref[c]
            for k in range(chunks):
                scr[slot, k, pl.ds(c, rows, stride=dil), :] = v[:, k * LANES:(k + 1) * LANES]
        return jnp.concatenate([scr[slot, k] for k in range(chunks)], axis=1)

    la, lb, lc = token_order(l0, 0), token_order(l1, 0), token_order(l2, 1)
    mx = jnp.maximum(jnp.maximum(la, lb), lc)
    ea, eb, ec = jnp.exp(la - mx), jnp.exp(lb - mx), jnp.exp(lc - mx)
    den = ea + eb + ec
    a = (ea / den) * token_order(o0, 2) + (eb / den) * token_order(o1, 2) + (ec / den) * token_order(o2, 3)
    y = _dot(a.astype(BF16), w_ref[...])
    out_ref[...] = x_ref[...] + _rms(y) * g_ref[...]


def _post_merge(outs, lses, w, x, gain, seq, *, tm=512):
    n, d = x.shape
    k = w.shape[0]
    per_seq = seq // tm

    def cm_spec(a):
        dil = a.shape[1]
        return pl.BlockSpec((None, dil, tm // dil, k), lambda i: (i // per_seq, 0, i % per_seq, 0))

    return pl.pallas_call(
        _post_merge_kernel,
        out_shape=jax.ShapeDtypeStruct((n, d), F32),
        grid=(n // tm,),
        in_specs=[cm_spec(a) for a in outs + lses] + [
            pl.BlockSpec((k, d), lambda i: (0, 0)),
            pl.BlockSpec((tm, d), lambda i: (i, 0)),
            pl.BlockSpec((1, d), lambda i: (0, 0)),
        ],
        out_specs=pl.BlockSpec((tm, d), lambda i: (i, 0)),
        scratch_shapes=[pltpu.VMEM((4, k // LANES, tm, LANES), F32)],
        compiler_params=_cparams("parallel"),
        name="post_merge",
    )(*outs, *lses, w, x, gain.reshape(1, d))


def _ffn_tail(conv, b, wd_ref, acc_scr, x_ref, g3_ref, o_ref):
    j = pl.program_id(1)
    act = jax.nn.gelu(conv, approximate=True) * b
    acc_scr[...] += _dot(act.astype(BF16), wd_ref[...])

    @pl.when(j == pl.num_programs(1) - 1)
    def _():
        o_ref[...] = x_ref[...] + _rms(acc_scr[...]) * g3_ref[...]


def _ffn_prompt_kernel(x_ref, halo_ref, g2_ref, g3_ref, wa_ref, wb_ref, cw_ref, cb_ref, wd_ref,
                       o_ref, tail_ref, h_scr, acc_scr, *, tiles_per_seq):
    i = pl.program_id(0)
    j = pl.program_id(1)
    hr = halo_ref.shape[0]
    tm = x_ref.shape[0]

    @pl.when(j == 0)
    def _():
        keep = jnp.where(i % tiles_per_seq != 0, 1.0, 0.0)
        h_scr[:hr, :] = (_rms(halo_ref[...]) * g2_ref[...] * keep).astype(BF16)
        h_scr[hr:, :] = (_rms(x_ref[...]) * g2_ref[...]).astype(BF16)
        acc_scr[...] = jnp.zeros_like(acc_scr)

    a_ext = _dot(h_scr[...], wa_ref[...])
    b = _dot(h_scr[hr:, :], wb_ref[...])
    a = a_ext[hr:, :]
    prev1 = a_ext[hr - 1:hr, :]
    prev2 = a_ext[hr - 2:hr - 1, :]
    row = lax.broadcasted_iota(jnp.int32, a.shape, 0)
    r1 = jnp.where(row == 0, prev1, pltpu.roll(a, 1, 0))
    r2 = jnp.where(row == 0, prev2, jnp.where(row == 1, prev1, pltpu.roll(a, 2, 0)))
    cw = cw_ref[...]
    conv = cb_ref[...] + ((r2 * cw[0:1, :] + r1 * cw[1:2, :]) + a * cw[2:3, :])
    tail_ref[...] = a[tm - 8:, :]
    _ffn_tail(conv, b, wd_ref, acc_scr, x_ref, g3_ref, o_ref)


def _ffn_sample_kernel(x_ref, p1_ref, p2_ref, g2_ref, g3_ref, wa_ref, wb_ref, cw_ref, cb_ref, wd_ref,
                       o_ref, a_ref, h_scr, acc_scr, *, seq):
    j = pl.program_id(1)

    @pl.when(j == 0)
    def _():
        h_scr[...] = (_rms(x_ref[...]) * g2_ref[...]).astype(BF16)
        acc_scr[...] = jnp.zeros_like(acc_scr)

    a = _dot(h_scr[...], wa_ref[...])
    b = _dot(h_scr[...], wb_ref[...])
    pos = lax.broadcasted_iota(jnp.int32, a.shape, 0) & (seq - 1)
    r1 = jnp.where(pos == 0, p1_ref[...], pltpu.roll(a, 1, 0))
    r2 = jnp.where(pos < 2, p2_ref[...], pltpu.roll(a, 2, 0))
    cw = cw_ref[...]
    conv = cb_ref[...] + ((r2 * cw[0:1, :] + r1 * cw[1:2, :]) + a * cw[2:3, :])
    a_ref[...] = a
    _ffn_tail(conv, b, wd_ref, acc_scr, x_ref, g3_ref, o_ref)


def _ffn_common_specs(d, dff, tf):
    nf = dff // tf
    return [
        pl.BlockSpec((1, d), lambda i, j: (0, 0)),
        pl.BlockSpec((1, d), lambda i, j: (0, 0)),
        pl.BlockSpec((d, tf), lambda i, j: (0, j)),
        pl.BlockSpec((d, tf), lambda i, j: (0, j + nf)),
        pl.BlockSpec((3, tf), lambda i, j: (0, j)),
        pl.BlockSpec((1, tf), lambda i, j: (0, j)),
        pl.BlockSpec((tf, d), lambda i, j: (j, 0)),
    ]


def _ffn_prompt(x, seq_len, g2, g3, w_up, conv_w, conv_b, w_down, *, tm=512, tf=512):
    n, d = x.shape
    dff = w_down.shape[0]
    nt = n // tm
    hr = BF16_ROWS
    in_specs = [
        pl.BlockSpec((tm, d), lambda i, j: (i, 0)),
        pl.BlockSpec((hr, d), lambda i, j: (jnp.maximum(i * (tm // hr) - 1, 0), 0)),
    ] + _ffn_common_specs(d, dff, tf)
    y, tail = pl.pallas_call(
        functools.partial(_ffn_prompt_kernel, tiles_per_seq=seq_len // tm),
        out_shape=[jax.ShapeDtypeStruct((n, d), F32), jax.ShapeDtypeStruct((nt * 8, dff), F32)],
        grid=(nt, dff // tf),
        in_specs=in_specs,
        out_specs=[pl.BlockSpec((tm, d), lambda i, j: (i, 0)), pl.BlockSpec((8, tf), lambda i, j: (i, j))],
        scratch_shapes=[pltpu.VMEM((tm + hr, d), BF16), pltpu.VMEM((tm, d), F32)],
        compiler_params=_cparams("parallel", "arbitrary"),
        name="ffn_prompt",
    )(x, x, g2.reshape(1, d), g3.reshape(1, d), w_up, w_up, conv_w, conv_b.reshape(1, dff), w_down)
    return y, tail


def _ffn_sample(x, seq, p1, p2, g2, g3, w_up, conv_w, conv_b, w_down, *, tm=512, tf=512):
    n, d = x.shape
    dff = w_down.shape[0]
    in_specs = [
        pl.BlockSpec((tm, d), lambda i, j: (i, 0)),
        pl.BlockSpec((tm, tf), lambda i, j: (i, j)),
        pl.BlockSpec((tm, tf), lambda i, j: (i, j)),
    ] + _ffn_common_specs(d, dff, tf)
    y, a = pl.pallas_call(
        functools.partial(_ffn_sample_kernel, seq=seq),
        out_shape=[jax.ShapeDtypeStruct((n, d), F32), jax.ShapeDtypeStruct((n, dff), F32)],
        grid=(n // tm, dff // tf),
        in_specs=in_specs,
        out_specs=[pl.BlockSpec((tm, d), lambda i, j: (i, 0)), pl.BlockSpec((tm, tf), lambda i, j: (i, j))],
        scratch_shapes=[pltpu.VMEM((tm, d), BF16), pltpu.VMEM((tm, d), F32)],
        compiler_params=_cparams("parallel", "arbitrary"),
        name="ffn_sample",
    )(x, p1, p2, g2.reshape(1, d), g3.reshape(1, d), w_up, w_up, conv_w, conv_b.reshape(1, dff), w_down)
    return y, a


def _attn_prompt_kernel(q_ref, kp_ref, kc_ref, vp_ref, vc_ref, o_ref, l_ref):
    i = pl.program_id(2)
    tq = q_ref.shape[0]
    qi = lax.broadcasted_iota(jnp.int32, (tq, 2 * tq), 0) + tq
    ki = lax.broadcasted_iota(jnp.int32, (tq, 2 * tq), 1)
    dist = qi - ki
    first_key = jnp.where(i > 0, 0, tq)
    valid = (dist >= 0) & (dist <= A_STEPS) & (ki >= first_key)
    scale = A_HEAD_DIM ** -0.5
    for h in range(A_HEADS):
        hs = slice(h * A_HEAD_DIM, (h + 1) * A_HEAD_DIM)
        q = q_ref[:, hs].astype(BF16)
        k = jnp.concatenate([kp_ref[:, hs], kc_ref[:, hs]], axis=0).astype(BF16)
        v = jnp.concatenate([vp_ref[:, hs], vc_ref[:, hs]], axis=0).astype(BF16)
        s = jnp.where(valid, _dot_nt(q, k) * scale, NEG_INF)
        m = jnp.max(s, axis=-1, keepdims=True)
        p = jnp.exp(s - m)
        l = jnp.sum(p, axis=-1, keepdims=True)
        o_ref[:, hs] = _dot(p.astype(BF16), v) / l
        l_ref[:, hs] = jnp.broadcast_to(m + jnp.log(l), (tq, A_HEAD_DIM))


def _attn_prompt(qkv, batch, seq):
    dil, n = qkv.shape[1], qkv.shape[2]
    tq = A_STEPS

    def spec(role, prev):
        def imap(b, c, i):
            return (b, c, jnp.maximum(i - 1, 0) if prev else i, role)
        return pl.BlockSpec((None, None, tq, A_WIDTH), imap)

    out_spec = pl.BlockSpec((None, None, tq, A_WIDTH), lambda b, c, i: (b, c, i, 0))
    return pl.pallas_call(
        _attn_prompt_kernel,
        out_shape=[jax.ShapeDtypeStruct((batch, dil, n, A_WIDTH), F32)] * 2,
        grid=(batch, dil, n // tq),
        in_specs=[spec(0, False), spec(1, True), spec(1, False), spec(2, True), spec(2, False)],
        out_specs=[out_spec, out_spec],
        compiler_params=_cparams("parallel", "parallel", "arbitrary"),
        name="attn_prompt_d%d" % dil,
    )(qkv, qkv, qkv, qkv, qkv)


def _attn_sample_kernel(new_ref, c0_ref, c1_ref, c2_ref, o_ref):
    nrow = new_ref.shape[2]
    s_new = nrow // A_HEADS
    scale = A_HEAD_DIM ** -0.5
    row_n = lax.broadcasted_iota(jnp.int32, (nrow, nrow), 0)
    col_n = lax.broadcasted_iota(jnp.int32, (nrow, nrow), 1)
    age_n = (row_n & (s_new - 1)) - (col_n & (s_new - 1))
    s_shift = s_new.bit_length() - 1
    same_head_n = (row_n >> s_shift) == (col_n >> s_shift)

    outs, lses = [], []
    for g, ((_, dil), cref) in enumerate(zip(A_GROUPS, (c0_ref, c1_ref, c2_ref))):
        n_past = cref.shape[3]
        qf = new_ref[g, 0]
        kn = new_ref[g, 1].astype(BF16)
        vn = new_ref[g, 2].astype(BF16)
        valid_n = same_head_n & (age_n >= 0) & ((age_n & (dil - 1)) == 0)
        s_own = jnp.where(valid_n, _dot_nt(qf.astype(BF16), kn) * scale, NEG_INF)
        age = (lax.broadcasted_iota(jnp.int32, (s_new, n_past), 0)
               - lax.broadcasted_iota(jnp.int32, (s_new, n_past), 1))
        valid = (age <= 0) & ((age & (dil - 1)) == 0)
        o_heads, l_heads = [], []
        for h in range(A_HEADS):
            rows = slice(h * s_new, (h + 1) * s_new)
            q = qf[rows, :].astype(BF16)
            s_buf = jnp.where(valid, _dot(q, cref[0, h].astype(BF16)) * scale, NEG_INF)
            s_new_h = s_own[rows, :]
            m = jnp.maximum(jnp.max(s_buf, axis=-1, keepdims=True), jnp.max(s_new_h, axis=-1, keepdims=True))
            p_buf = jnp.exp(s_buf - m)
            p_own = jnp.exp(s_new_h - m)
            l = jnp.sum(p_buf, axis=-1, keepdims=True) + jnp.sum(p_own, axis=-1, keepdims=True)
            o = _dot_nt(p_buf.astype(BF16), cref[1, h].astype(BF16)) + _dot(p_own.astype(BF16), vn)
            o_heads.append(o / l)
            l_heads.append(m + jnp.log(l))
        outs.append(jnp.concatenate(o_heads, axis=0))
        lses.append(jnp.concatenate(l_heads, axis=0))

    mx = jnp.maximum(jnp.maximum(lses[0], lses[1]), lses[2])
    es = [jnp.exp(x - mx) for x in lses]
    den = es[0] + es[1] + es[2]
    o_ref[...] = (es[0] / den) * outs[0] + (es[1] / den) * outs[1] + (es[2] / den) * outs[2]


def _attn_sample(qkv, caches, j, batch, s_new):
    new = qkv.reshape(batch, s_new, len(A_GROUPS), 3, A_HEADS, A_HEAD_DIM)
    new = new.transpose(0, 2, 3, 4, 1, 5).reshape(batch, len(A_GROUPS), 3, A_HEADS * s_new, A_HEAD_DIM)
    views = []
    specs = [pl.BlockSpec((None,) + new.shape[1:], lambda b: (b, 0, 0, 0, 0))]
    for (win, dil), cache in zip(A_GROUPS, caches):
        n_past = cache.shape[2]
        assert n_past == dil * A_STEPS
        views.append(cache.transpose(0, 1, 3, 4, 5, 2))
        specs.append(pl.BlockSpec((None, None, 2, A_HEADS, A_HEAD_DIM, n_past), lambda b: (j, b, 0, 0, 0, 0)))
    out = pl.pallas_call(
        _attn_sample_kernel,
        out_shape=jax.ShapeDtypeStruct((batch, A_HEADS * s_new, A_HEAD_DIM), F32),
        grid=(batch,),
        in_specs=specs,
        out_specs=pl.BlockSpec((None, A_HEADS * s_new, A_HEAD_DIM), lambda b: (b, 0, 0)),
        compiler_params=_cparams("parallel"),
        name="attn_sample",
    )(new, *views)
    out = out.reshape(batch, A_HEADS, s_new, A_HEAD_DIM).transpose(0, 2, 1, 3)
    return out.reshape(batch * s_new, A_WIDTH)


def _retention_kernel(q_ref, k_ref, v_ref, g_ref, s0_ref, dec_ref, inn_ref, tail_ref, cd_ref,
                      o_ref, sout_ref, s_scr):
    c = pl.program_id(1)
    lk = dec_ref.shape[2]

    @pl.when(c == 0)
    def _():
        s_scr[...] = s0_ref[...]

    for h in range(R_HEADS):
        ks = slice(h * R_DK, (h + 1) * R_DK)
        vs = slice(h * R_DV, (h + 1) * R_DV)
        q = q_ref[:, ks].astype(BF16)
        kf = _pad_rows(k_ref[:, ks] * (R_DK ** -0.5), lk)
        vb = _pad_rows(v_ref[:, vs], lk).astype(BF16)
        s = _dot_nt(q, kf.astype(BF16)) * dec_ref[h]
        state = s_scr[h]
        o = _dot(s.astype(BF16), vb) + _dot(q, state.astype(BF16)) * inn_ref[h]
        s_scr[h] = state * cd_ref[h] + _dot_tn((kf * tail_ref[h]).astype(BF16), vb)
        o_ref[:, vs] = (jax.nn.silu(g_ref[:, vs]) * _rms(o)).astype(o_ref.dtype)

    @pl.when(c == pl.num_programs(1) - 1)
    def _():
        sout_ref[...] = s_scr[...]


def _retention(proj, state, batch, seq):
    chunk = CHUNK if seq % CHUNK == 0 else seq
    lk = max(chunk, LANES)
    nc = seq // chunk
    nq, nv = R_HEADS * R_DK, R_HEADS * R_DV
    log_g = jnp.log1p(-jnp.exp2(-5.0 - jnp.arange(R_HEADS, dtype=F32)))
    i = jnp.arange(chunk, dtype=F32)
    diff = i[:, None] - i[None, :]
    decay = jnp.where(diff >= 0, jnp.exp(jnp.maximum(diff, 0.0)[None] * log_g[:, None, None]), 0.0)
    decay = jnp.pad(decay, ((0, 0), (0, 0), (0, lk - chunk)))
    inner = jnp.exp((i + 1.0)[None, :] * log_g[:, None])[:, :, None]
    tail = jnp.exp((chunk - 1.0 - i)[None, :] * log_g[:, None])
    tail = jnp.pad(tail, ((0, 0), (0, lk - chunk)))[:, :, None]
    cdec = jnp.exp(chunk * log_g)[:, None, None]
    out_dtype = BF16 if chunk % BF16_ROWS == 0 else F32
    whole = lambda b, c: (0, 0, 0)
    gated, new_state = pl.pallas_call(
        _retention_kernel,
        out_shape=[jax.ShapeDtypeStruct((batch * seq, nv), out_dtype),
                   jax.ShapeDtypeStruct(state.shape, F32)],
        grid=(batch, nc),
        in_specs=[
            pl.BlockSpec((chunk, nq), lambda b, c: (b * nc + c, 0)),
            pl.BlockSpec((chunk, nq), lambda b, c: (b * nc + c, 1)),
            pl.BlockSpec((chunk, nv), lambda b, c: (b * nc + c, 1)),
            pl.BlockSpec((chunk, nv), lambda b, c: (b * nc + c, 2)),
            pl.BlockSpec((None, R_HEADS, R_DK, R_DV), lambda b, c: (b, 0, 0, 0)),
            pl.BlockSpec((R_HEADS, chunk, lk), whole),
            pl.BlockSpec((R_HEADS, chunk, 1), whole),
            pl.BlockSpec((R_HEADS, lk, 1), whole),
            pl.BlockSpec((R_HEADS, 1, 1), whole),
        ],
        out_specs=[pl.BlockSpec((chunk, nv), lambda b, c: (b * nc + c, 0)),
                   pl.BlockSpec((None, R_HEADS, R_DK, R_DV), lambda b, c: (b, 0, 0, 0))],
        scratch_shapes=[pltpu.VMEM((R_HEADS, R_DK, R_DV), F32)],
        compiler_params=_cparams("parallel", "arbitrary"),
        name="retention",
    )(proj, proj, proj, proj, state, decay, inner, tail, cdec)
    return gated, new_state


def _mlstm_kernel(q_ref, k_ref, v_ref, og_ref, gt_ref, bias_ref, ng_ref, c0_ref, n0_ref, m0_ref,
                  o_ref, cout_ref, nout_ref, mout_ref, c_scr, n_scr, m_scr):
    c = pl.program_id(1)
    chunk = q_ref.shape[0]
    lk = max(chunk, LANES)

    @pl.when(c == 0)
    def _():
        c_scr[...] = c0_ref[...]
        n_scr[...] = n0_ref[...]
        m_scr[...] = m0_ref[...]

    pre = gt_ref[...] + bias_ref[...]
    log_f = -(jnp.maximum(-pre, 0.0) + jnp.log1p(jnp.exp(-jnp.abs(pre))))
    rowi = lax.broadcasted_iota(jnp.int32, pre.shape, 0)
    lane = lax.broadcasted_iota(jnp.int32, pre.shape, 1)
    cum = log_f
    d = 1
    while d < chunk:
        cum = cum + jnp.where(rowi >= d, pltpu.roll(cum, d, 0), 0.0)
        d *= 2
    cols = jnp.where(lane < M_HEADS, pre, cum)
    rows = _pad_rows(cols, lk).T
    t_idx = lax.broadcasted_iota(jnp.int32, (chunk, lk), 0)
    s_idx = lax.broadcasted_iota(jnp.int32, (chunk, lk), 1)
    causal = s_idx <= t_idx

    for h in range(M_HEADS):
        ks = slice(h * M_DK, (h + 1) * M_DK)
        vs = slice(h * M_DV, (h + 1) * M_DV)
        i_row = rows[h:h + 1, :]
        b_row = rows[M_HEADS + h:M_HEADS + h + 1, :]
        i_col = cols[:, h:h + 1]
        b_col = cols[:, M_HEADS + h:M_HEADS + h + 1]
        m_prev = m_scr[h:h + 1, 0:1]
        logw = jnp.where(causal, (b_col - b_row) + i_row, NEG_INF)
        inter = b_col + m_prev
        m_t = jnp.maximum(inter, jnp.max(logw, axis=-1, keepdims=True))
        w = jnp.exp(logw - m_t)
        a = jnp.exp(inter - m_t)

        qf = q_ref[:, ks]
        q = qf.astype(BF16)
        kf = k_ref[:, ks] * (M_DK ** -0.5)
        vb = _pad_rows(v_ref[:, vs], lk).astype(BF16)
        qk = _dot_nt(q, _pad_rows(kf, lk).astype(BF16)) * w
        cmat = c_scr[h]
        nrow = n_scr[h:h + 1, :]
        num = _dot(qk.astype(BF16), vb) + a * _dot(q, cmat.astype(BF16))
        den = jnp.sum(qk, axis=-1, keepdims=True) + a * jnp.sum(qf * nrow, axis=-1, keepdims=True)
        hid = num / jnp.maximum(jnp.abs(den), jnp.exp(-m_t))

        b_end = b_col[chunk - 1:chunk, :]
        m_end = m_t[chunk - 1:chunk, :]
        a_end = a[chunk - 1:chunk, :]
        kw = kf * jnp.exp(((b_end - b_col) + i_col) - m_end)
        c_scr[h] = a_end * cmat + _dot_tn(_pad_rows(kw, lk).astype(BF16), vb)
        n_scr[h:h + 1, :] = a_end * nrow + jnp.sum(kw, axis=0, keepdims=True)
        m_scr[h:h + 1, :] = jnp.broadcast_to(m_end, (1, m_scr.shape[1]))

        y = _rms(hid) * ng_ref[h:h + 1, :]
        o_ref[:, vs] = (jax.nn.sigmoid(og_ref[:, vs]) * y).astype(o_ref.dtype)

    @pl.when(c == pl.num_programs(1) - 1)
    def _():
        cout_ref[...] = c_scr[...]
        nout_ref[...] = n_scr[...]
        mout_ref[...] = m_scr[...]


def _mlstm(proj, gates, bias_row, norm_g, c0, n0, m0, batch, seq):
    chunk = CHUNK if seq % CHUNK == 0 else seq
    nc = seq // chunk
    nq, nv = M_HEADS * M_DK, M_HEADS * M_DV
    m0b = jnp.broadcast_to(m0[:, :, None], (batch, M_HEADS, LANES))
    out_dtype = BF16 if chunk % BF16_ROWS == 0 else F32
    st4 = lambda b, c: (b, 0, 0, 0)
    st3 = lambda b, c: (b, 0, 0)
    gated, c_new, n_new, m_new = pl.pallas_call(
        _mlstm_kernel,
        out_shape=[jax.ShapeDtypeStruct((batch * seq, nv), out_dtype),
                   jax.ShapeDtypeStruct(c0.shape, F32),
                   jax.ShapeDtypeStruct(n0.shape, F32),
                   jax.ShapeDtypeStruct(m0b.shape, F32)],
        grid=(batch, nc),
        in_specs=[
            pl.BlockSpec((chunk, nq), lambda b, c: (b * nc + c, 0)),
            pl.BlockSpec((chunk, nq), lambda b, c: (b * nc + c, 1)),
            pl.BlockSpec((chunk, nv), lambda b, c: (b * nc + c, 1)),
            pl.BlockSpec((chunk, nv), lambda b, c: (b * nc + c, 2)),
            pl.BlockSpec((chunk, LANES), lambda b, c: (b * nc + c, 0)),
            pl.BlockSpec((1, LANES), lambda b, c: (0, 0)),
            pl.BlockSpec((M_HEADS, M_DV), lambda b, c: (0, 0)),
            pl.BlockSpec((None, M_HEADS, M_DK, M_DV), st4),
            pl.BlockSpec((None, M_HEADS, M_DK), st3),
            pl.BlockSpec((None, M_HEADS, LANES), st3),
        ],
        out_specs=[pl.BlockSpec((chunk, nv), lambda b, c: (b * nc + c, 0)),
                   pl.BlockSpec((None, M_HEADS, M_DK, M_DV), st4),
                   pl.BlockSpec((None, M_HEADS, M_DK), st3),
                   pl.BlockSpec((None, M_HEADS, LANES), st3)],
        scratch_shapes=[pltpu.VMEM((M_HEADS, M_DK, M_DV), F32),
                        pltpu.VMEM((M_HEADS, M_DK), F32),
                        pltpu.VMEM((M_HEADS, LANES), F32)],
        compiler_params=_cparams("parallel", "arbitrary"),
        name="mlstm",
    )(proj, proj, proj, proj, gates, bias_row, norm_g, c0, n0, m0b)
    return gated, c_new, n_new, m_new[:, :, 0]


def _rope_tables_a(pos):
    inv = ROPE_THETA ** (-jnp.arange(ROPE_DIM // 2, dtype=F32) * (2.0 / ROPE_DIM))
    ang = pos.astype(F32)[:, None] * inv[None, :]
    cos, sin = jnp.cos(ang), jnp.sin(ang)
    half = ROPE_DIM // 2
    n = pos.shape[0]
    pad = A_HEAD_DIM - ROPE_DIM
    c = jnp.concatenate([cos, cos, jnp.ones((n, pad), F32)], axis=1)
    sp = jnp.concatenate([jnp.zeros((n, half), F32), sin, jnp.zeros((n, pad), F32)], axis=1)
    sm = jnp.concatenate([-sin, jnp.zeros((n, half + pad), F32)], axis=1)
    reps = LANES // A_HEAD_DIM
    return tuple(jnp.tile(t, (1, reps)) for t in (c, sp, sm))


def _rope_tables_b(pos):
    inv = R_THETA ** (-jnp.linspace(0.0, 1.0, R_DK // 2, dtype=F32))
    ang = pos.astype(F32)[:, None] * inv[None, :]
    return jnp.cos(ang), jnp.sin(ang)


def kernel(x_prompt, x_sample, cache_win128_kv, cache_win512_kv, cache_win2048_kv, state_ret,
           state_mlstm_c, state_mlstm_n, state_mlstm_m, state_ffn_conv, norm_gains,
           a_w_in, a_w_out, b_w_in, b_w_out, c_w_in, c_b_if, c_norm, c_w_out,
           f_w_up, f_conv_w, f_conv_b, f_w_down):
    bp, t, d = x_prompt.shape
    bs, s, _ = x_sample.shape
    depth = norm_gains.shape[0]
    caches = (cache_win128_kv, cache_win512_kv, cache_win2048_kv)
    pos_p = jnp.arange(t, dtype=jnp.int32)
    pos_s = jnp.tile(PAST_LEN + jnp.arange(s, dtype=jnp.int32), bs)
    tabs_a_p, tabs_a_s = _rope_tables_a(pos_p), _rope_tables_a(pos_s)
    tabs_b_p, tabs_b_s = _rope_tables_b(pos_p), _rope_tables_b(pos_s)
    tm = 512
    tm_p = 1024

    xp = x_prompt.reshape(bp * t, d)
    xs = x_sample.reshape(bs * s, d)
    win_p = [[] for _ in A_GROUPS]
    win_s = [[] for _ in A_GROUPS]
    ret_p, ret_s = [], []
    c_p, c_s, n_p, n_s, m_p, m_s = [], [], [], [], [], []
    conv_p, conv_s = [], []

    for layer in range(depth):
        kind, j = layer % N_MIXERS, layer // N_MIXERS
        g = norm_gains[layer]
        if kind == 0:
            w_in = a_w_in[j].astype(BF16)
            w_out = a_w_out[j].astype(BF16)
            qkv_p = _proj_a_prompt(xp, g[0], w_in, tabs_a_p, bp, t, tm=tm_p)
            qkv_s = _norm_proj(xs, g[0], w_in, mode="rope_a", tabs=tabs_a_s, tab_blocks=bs * s // tm, tm=tm)
            outs, lses = [], []
            for gi, (win, dil) in enumerate(A_GROUPS):
                o, l = _attn_prompt(qkv_p[gi], bp, t)
                outs.append(o)
                lses.append(l)
                keep = min(win, t)
                kv = qkv_p[gi][:, :, (t - keep) // dil:, A_WIDTH:].transpose(0, 2, 1, 3)
                win_p[gi].append(kv.reshape(bp, keep, 2, A_HEADS, A_HEAD_DIM))
                lo = gi * 3 * A_WIDTH + A_WIDTH
                kv = qkv_s.reshape(bs, s, -1)[:, :, lo:lo + 2 * A_WIDTH]
                win_s[gi].append(kv.reshape(bs, s, 2, A_HEADS, A_HEAD_DIM))
            xp = _post_merge(outs, lses, w_out, xp, g[1], t, tm=tm)
            att_s = _attn_sample(qkv_s, caches, j, bs, s)
            xs = _post(att_s, w_out, xs, g[1], tm=tm)
        elif kind == 1:
            w_in = b_w_in[j].astype(BF16)
            w_out = b_w_out[j].astype(BF16)
            proj_p = _norm_proj(xp, g[0], w_in, mode="rope_b", tabs=tabs_b_p, tab_blocks=t // tm_p, tm=tm_p)
            proj_s = _norm_proj(xs, g[0], w_in, mode="rope_b", tabs=tabs_b_s, tab_blocks=bs * s // tm, tm=tm)
            gated_p, sp_new = _retention(proj_p, jnp.zeros((bp, R_HEADS, R_DK, R_DV), F32), bp, t)
            gated_s, ss_new = _retention(proj_s, state_ret[j], bs, s)
            ret_p.append(sp_new)
            ret_s.append(ss_new)
            xp = _post(gated_p, w_out, xp, g[1], tm=tm)
            xs = _post(gated_s, w_out, xs, g[1], tm=tm)
        else:
            n_main = 2 * M_HEADS * M_DK + 2 * M_HEADS * M_DV
            w_in = c_w_in[j][:, :n_main].astype(BF16)
            w_g = jnp.pad(c_w_in[j][:, n_main:], ((0, 0), (0, LANES - 2 * M_HEADS))).astype(BF16)
            w_out = c_w_out[j].astype(BF16)
            bias_row = jnp.pad(c_b_if[j].reshape(1, 2 * M_HEADS), ((0, 0), (0, LANES - 2 * M_HEADS)))
            proj_p, gates_p = _norm_proj(xp, g[0], w_in, mode="gates", wg=w_g, tm=tm_p)
            proj_s, gates_s = _norm_proj(xs, g[0], w_in, mode="gates", wg=w_g, tm=tm)
            zc = jnp.zeros((bp, M_HEADS, M_DK, M_DV), F32)
            zn = jnp.zeros((bp, M_HEADS, M_DK), F32)
            zm = jnp.zeros((bp, M_HEADS), F32)
            gated_p, cp, np_, mp = _mlstm(proj_p, gates_p, bias_row, c_norm[j], zc, zn, zm, bp, t)
            gated_s, cs, ns, ms = _mlstm(proj_s, gates_s, bias_row, c_norm[j], state_mlstm_c[j],
                                         state_mlstm_n[j], state_mlstm_m[j], bs, s)
            c_p.append(cp); c_s.append(cs)
            n_p.append(np_); n_s.append(ns)
            m_p.append(mp); m_s.append(ms)
            xp = _post(gated_p, w_out, xp, g[1], tm=tm)
            xs = _post(gated_s, w_out, xs, g[1], tm=tm)

        w_up = f_w_up[layer].astype(BF16)
        w_down = f_w_down[layer].astype(BF16)
        xp, tail = _ffn_prompt(xp, t, g[2], g[3], w_up, f_conv_w[layer], f_conv_b[layer], w_down, tm=tm_p)
        tail = tail.reshape(bp, t // tm_p, 8, D_FF)
        conv_p.append(tail[:, -1, 6:, :])
        buf = state_ffn_conv[layer]
        zrow = jnp.zeros((bs, s - 1, D_FF), F32)
        p1 = jnp.concatenate([buf[:, 1:2], zrow], axis=1).reshape(bs * s, D_FF)
        p2 = jnp.concatenate([buf, zrow[:, 1:]], axis=1).reshape(bs * s, D_FF)
        xs, a_s = _ffn_sample(xs, s, p1, p2, g[2], g[3], w_up, f_conv_w[layer], f_conv_b[layer], w_down, tm=tm)
        conv_s.append(a_s.reshape(bs, s, D_FF)[:, s - 2:, :])

    return (xp.reshape(bp, t, d), xs.reshape(bs, s, d),
            jnp.stack(win_p[0]), jnp.stack(win_s[0]),
            jnp.stack(win_p[1]), jnp.stack(win_s[1]),
            jnp.stack(win_p[2]), jnp.stack(win_s[2]),
            jnp.stack(ret_p), jnp.stack(ret_s),
            jnp.stack(c_p), jnp.stack(c_s),
            jnp.stack(n_p), jnp.stack(n_s),
            jnp.stack(m_p), jnp.stack(m_s),
            jnp.stack(conv_p), jnp.stack(conv_s))
```

```python
import functools

import jax
import jax.numpy as jnp
from jax import lax
from jax.experimental import pallas as pl
from jax.experimental.pallas import tpu as pltpu

F32 = jnp.float32
BF16 = jnp.bfloat16

D_MODEL = 1024
PAST_LEN = 2048
N_MIXERS = 3
A_GROUPS = ((128, 1), (512, 4), (2048, 16))
A_HEADS = 8
A_HEAD_DIM = 64
A_WIDTH = A_HEADS * A_HEAD_DIM
A_STEPS = 128
ROPE_THETA = 500000.0
ROPE_DIM = A_HEAD_DIM // 4
R_HEADS = 4
R_DK = 256
R_DV = 512
R_THETA = 10000.0
M_HEADS = 4
M_DK = 256
M_DV = 512
CHUNK = 128
D_FF = 4 * D_MODEL
EPS = 1e-6

LANES = 128
BF16_ROWS = 16
VMEM_LIMIT = 52 * 1024 * 1024

NT_DIMS = (((1,), (1,)), ((), ()))
TN_DIMS = (((0,), (0,)), ((), ()))
NEG_INF = float("-inf")


def _cparams(*sem):
    return pltpu.CompilerParams(dimension_semantics=sem, vmem_limit_bytes=VMEM_LIMIT)


def _rms(x):
    return x * lax.rsqrt(jnp.mean(x * x, axis=-1, keepdims=True) + EPS)


def _dot(a, b):
    return jnp.dot(a, b, preferred_element_type=F32)


def _dot_nt(a, b):
    return lax.dot_general(a, b, NT_DIMS, preferred_element_type=F32)


def _dot_tn(a, b):
    return lax.dot_general(a, b, TN_DIMS, preferred_element_type=F32)


def _pad_rows(x, rows):
    if x.shape[0] == rows:
        return x
    return jnp.concatenate([x, jnp.zeros((rows - x.shape[0], x.shape[1]), x.dtype)], axis=0)


def _norm_proj_kernel(*refs, mode):
    if mode == "rope_a":
        x_ref, g_ref, w_ref, c_ref, sp_ref, sm_ref, o_ref, h_scr = refs
    elif mode == "rope_b":
        x_ref, g_ref, w_ref, c_ref, s_ref, o_ref, h_scr = refs
    else:
        x_ref, g_ref, w_ref, wg_ref, o_ref, og_ref, h_scr = refs
    j = pl.program_id(1)

    @pl.when(j == 0)
    def _():
        h_scr[...] = (_rms(x_ref[...]) * g_ref[...]).astype(BF16)
        if mode == "gates":
            og_ref[...] = _dot(h_scr[...], wg_ref[...])

    acc = _dot(h_scr[...], w_ref[...])
    if mode == "rope_a":
        @pl.when(j % 3 != 2)
        def _():
            o_ref[...] = _rope_a(acc, c_ref, sp_ref, sm_ref)

        @pl.when(j % 3 == 2)
        def _():
            o_ref[...] = acc
    elif mode == "rope_b":
        @pl.when(j < 4)
        def _():
            c = c_ref[...]
            s = s_ref[...]
            outs = []
            for hh in range(acc.shape[1] // R_DK):
                x1 = acc[:, hh * R_DK: hh * R_DK + R_DK // 2]
                x2 = acc[:, hh * R_DK + R_DK // 2: (hh + 1) * R_DK]
                outs += [x1 * c - x2 * s, x2 * c + x1 * s]
            o_ref[...] = jnp.concatenate(outs, axis=1)

        @pl.when(j >= 4)
        def _():
            o_ref[...] = acc
    else:
        o_ref[...] = acc


def _norm_proj(x, gain, w, *, mode, tabs=(), tab_blocks=1, wg=None, tm=512, tn=512):
    n, d = x.shape
    m = (w.shape[1] // tn) * tn
    grid = (n // tm, m // tn)
    in_specs = [
        pl.BlockSpec((tm, d), lambda i, j: (i, 0)),
        pl.BlockSpec((1, d), lambda i, j: (0, 0)),
        pl.BlockSpec((d, tn), lambda i, j: (0, j)),
    ]
    args = [x, gain.reshape(1, d), w]
    out_shape = [jax.ShapeDtypeStruct((n, m), F32)]
    out_specs = [pl.BlockSpec((tm, tn), lambda i, j: (i, j))]
    if mode in ("rope_a", "rope_b"):
        for t in tabs:
            in_specs.append(pl.BlockSpec((tm, LANES), lambda i, j: (i % tab_blocks, 0)))
            args.append(t)
    else:
        in_specs.append(pl.BlockSpec((d, LANES), lambda i, j: (0, 0)))
        args.append(wg)
        out_shape.append(jax.ShapeDtypeStruct((n, LANES), F32))
        out_specs.append(pl.BlockSpec((tm, LANES), lambda i, j: (i, 0)))
    outs = pl.pallas_call(
        functools.partial(_norm_proj_kernel, mode=mode),
        out_shape=out_shape,
        grid=grid,
        in_specs=in_specs,
        out_specs=out_specs,
        scratch_shapes=[pltpu.VMEM((tm, d), BF16)],
        compiler_params=_cparams("parallel", "arbitrary"),
        name="norm_proj_" + mode,
    )(*args)
    return outs if mode == "gates" else outs[0]


def _rope_a(acc, c_ref, sp_ref, sm_ref):
    reps = acc.shape[1] // LANES
    c = jnp.concatenate([c_ref[...]] * reps, axis=1)
    sp = jnp.concatenate([sp_ref[...]] * reps, axis=1)
    sm = jnp.concatenate([sm_ref[...]] * reps, axis=1)
    half = ROPE_DIM // 2
    up = pltpu.roll(acc, half, 1)
    down = pltpu.roll(acc, acc.shape[1] - half, 1)
    return acc * c + down * sm + up * sp


def _proj_a_prompt_kernel(x_ref, g_ref, w_ref, c_ref, sp_ref, sm_ref, o0_ref, o1_ref, o2_ref, h_scr, acc_scr):
    j = pl.program_id(1)
    tm = x_ref.shape[0]
    chunks = A_WIDTH // LANES

    @pl.when(j == 0)
    def _():
        h_scr[...] = (_rms(x_ref[...]) * g_ref[...]).astype(BF16)

    acc = _dot(h_scr[...], w_ref[...])

    def stash(val):
        for k in range(chunks):
            acc_scr[k] = val[:, k * LANES:(k + 1) * LANES]

    @pl.when(j % 3 != 2)
    def _():
        stash(_rope_a(acc, c_ref, sp_ref, sm_ref))

    @pl.when(j % 3 == 2)
    def _():
        stash(acc)

    for g, o_ref in enumerate((o0_ref, o1_ref, o2_ref)):
        dil = o_ref.shape[0]
        rows = tm // dil

        @pl.when(j // 3 == g)
        def _(o_ref=o_ref, dil=dil, rows=rows):
            for c in range(dil):
                o_ref[c] = jnp.concatenate(
                    [acc_scr[k, pl.ds(c, rows, stride=dil), :] for k in range(chunks)], axis=1)


def _proj_a_prompt(x, gain, w, tabs, batch, seq, *, tm=1024):
    n, d = x.shape
    per_seq = seq // tm
    out_shape, out_specs = [], []
    for g, (_, dil) in enumerate(A_GROUPS):
        out_shape.append(jax.ShapeDtypeStruct((batch, dil, seq // dil, 3 * A_WIDTH), F32))
        out_specs.append(pl.BlockSpec(
            (None, dil, tm // dil, A_WIDTH),
            lambda i, j, g=g: (i // per_seq, 0, i % per_seq, jnp.clip(j - 3 * g, 0, 2))))
    in_specs = [
        pl.BlockSpec((tm, d), lambda i, j: (i, 0)),
        pl.BlockSpec((1, d), lambda i, j: (0, 0)),
        pl.BlockSpec((d, A_WIDTH), lambda i, j: (0, j)),
    ] + [pl.BlockSpec((tm, LANES), lambda i, j: (i % per_seq, 0)) for _ in tabs]
    return pl.pallas_call(
        _proj_a_prompt_kernel,
        out_shape=out_shape,
        grid=(n // tm, w.shape[1] // A_WIDTH),
        in_specs=in_specs,
        out_specs=out_specs,
        scratch_shapes=[pltpu.VMEM((tm, d), BF16), pltpu.VMEM((A_WIDTH // LANES, tm, LANES), F32)],
        compiler_params=_cparams("parallel", "arbitrary"),
        name="proj_a_prompt",
    )(x, gain.reshape(1, d), w, *tabs)


def _post_kernel(a_ref, w_ref, x_ref, g_ref, out_ref):
    y = _dot(a_ref[...].astype(BF16), w_ref[...])
    out_ref[...] = x_ref[...] + _rms(y) * g_ref[...]


def _post(a, w, x, gain, *, tm=512):
    n, d = x.shape
    k = w.shape[0]
    return pl.pallas_call(
        _post_kernel,
        out_shape=jax.ShapeDtypeStruct((n, d), F32),
        grid=(n // tm,),
        in_specs=[
            pl.BlockSpec((tm, k), lambda i: (i, 0)),
            pl.BlockSpec((k, d), lambda i: (0, 0)),
            pl.BlockSpec((tm, d), lambda i: (i, 0)),
            pl.BlockSpec((1, d), lambda i: (0, 0)),
        ],
        out_specs=pl.BlockSpec((tm, d), lambda i: (i, 0)),
        compiler_params=_cparams("parallel"),
        name="post",
    )(a, w, x, gain.reshape(1, d))


def _post_merge_kernel(o0, o1, o2, l0, l1, l2, w_ref, x_ref, g_ref, out_ref, scr):
    tm = x_ref.shape[0]
    chunks = A_WIDTH // LANES

    def token_order(ref, slot):
        dil = ref.shape[0]
        if dil == 1:
            return ref[0]
        rows = tm // dil
        for c in range(dil):
            v = ref[c]
            for k in range(chunks):
                scr[slot, k, pl.ds(c, rows, stride=dil), :] = v[:, k * LANES:(k + 1) * LANES]
        return jnp.concatenate([scr[slot, k] for k in range(chunks)], axis=1)

    la, lb, lc = token_order(l0, 0), token_order(l1, 0), token_order(l2, 1)
    mx = jnp.maximum(jnp.maximum(la, lb), lc)
    ea, eb, ec = jnp.exp(la - mx), jnp.exp(lb - mx), jnp.exp(lc - mx)
    den = ea + eb + ec
    a = (ea / den) * token_order(o0, 2) + (eb / den) * token_order(o1, 2) + (ec / den) * token_order(o2, 3)
    y = _dot(a.astype(BF16), w_ref[...])
    out_ref[...] = x_ref[...] + _rms(y) * g_ref[...]


def _post_merge(outs, lses, w, x, gain, seq, *, tm=512):
    n, d = x.shape
    k = w.shape[0]
    per_seq = seq // tm

    def cm_spec(a):
        dil = a.shape[1]
        return pl.BlockSpec((None, dil, tm // dil, k), lambda i: (i // per_seq, 0, i % per_seq, 0))

    return pl.pallas_call(
        _post_merge_kernel,
        out_shape=jax.ShapeDtypeStruct((n, d), F32),
        grid=(n // tm,),
        in_specs=[cm_spec(a) for a in outs + lses] + [
            pl.BlockSpec((k, d), lambda i: (0, 0)),
            pl.BlockSpec((tm, d), lambda i: (i, 0)),
            pl.BlockSpec((1, d), lambda i: (0, 0)),
        ],
        out_specs=pl.BlockSpec((tm, d), lambda i: (i, 0)),
        scratch_shapes=[pltpu.VMEM((4, k // LANES, tm, LANES), F32)],
        compiler_params=_cparams("parallel"),
        name="post_merge",
    )(*outs, *lses, w, x, gain.reshape(1, d))


def _ffn_tail(conv, b, wd_ref, acc_scr, x_ref, g3_ref, o_ref):
    j = pl.program_id(1)
    act = jax.nn.gelu(conv, approximate=True) * b
    acc_scr[...] += _dot(act.astype(BF16), wd_ref[...])

    @pl.when(j == pl.num_programs(1) - 1)
    def _():
        o_ref[...] = x_ref[...] + _rms(acc_scr[...]) * g3_ref[...]


def _ffn_prompt_kernel(x_ref, halo_ref, g2_ref, g3_ref, wa_ref, wb_ref, cw_ref, cb_ref, wd_ref,
                       o_ref, tail_ref, h_scr, acc_scr, *, tiles_per_seq):
    i = pl.program_id(0)
    j = pl.program_id(1)
    hr = halo_ref.shape[0]
    tm = x_ref.shape[0]

    @pl.when(j == 0)
    def _():
        keep = jnp.where(i % tiles_per_seq != 0, 1.0, 0.0)
        h_scr[:hr, :] = (_rms(halo_ref[...]) * g2_ref[...] * keep).astype(BF16)
        h_scr[hr:, :] = (_rms(x_ref[...]) * g2_ref[...]).astype(BF16)
        acc_scr[...] = jnp.zeros_like(acc_scr)

    a_ext = _dot(h_scr[...], wa_ref[...])
    b = _dot(h_scr[hr:, :], wb_ref[...])
    a = a_ext[hr:, :]
    prev1 = a_ext[hr - 1:hr, :]
    prev2 = a_ext[hr - 2:hr - 1, :]
    r1, r2 = pltpu.roll(a, 1, 0), pltpu.roll(a, 2, 0)
    row = lax.broadcasted_iota(jnp.int32, (8, a.shape[1]), 0)
    r1 = jnp.concatenate([jnp.where(row == 0, prev1, r1[:8]), r1[8:]], axis=0)
    r2 = jnp.concatenate([jnp.where(row == 0, prev2, jnp.where(row == 1, prev1, r2[:8])), r2[8:]], axis=0)
    cw = cw_ref[...]
    conv = cb_ref[...] + ((r2 * cw[0:1, :] + r1 * cw[1:2, :]) + a * cw[2:3, :])
    tail_ref[...] = a[tm - 8:, :]
    _ffn_tail(conv, b, wd_ref, acc_scr, x_ref, g3_ref, o_ref)


def _ffn_sample_kernel(x_ref, p1_ref, p2_ref, g2_ref, g3_ref, wa_ref, wb_ref, cw_ref, cb_ref, wd_ref,
                       o_ref, a_ref, h_scr, acc_scr, *, seq):
    j = pl.program_id(1)

    @pl.when(j == 0)
    def _():
        h_scr[...] = (_rms(x_ref[...]) * g2_ref[...]).astype(BF16)
        acc_scr[...] = jnp.zeros_like(acc_scr)

    a = _dot(h_scr[...], wa_ref[...])
    b = _dot(h_scr[...], wb_ref[...])
    pos = lax.broadcasted_iota(jnp.int32, a.shape, 0) & (seq - 1)
    r1 = jnp.where(pos == 0, p1_ref[...], pltpu.roll(a, 1, 0))
    r2 = jnp.where(pos < 2, p2_ref[...], pltpu.roll(a, 2, 0))
    cw = cw_ref[...]
    conv = cb_ref[...] + ((r2 * cw[0:1, :] + r1 * cw[1:2, :]) + a * cw[2:3, :])
    a_ref[...] = a
    _ffn_tail(conv, b, wd_ref, acc_scr, x_ref, g3_ref, o_ref)


def _ffn_common_specs(d, dff, tf):
    nf = dff // tf
    return [
        pl.BlockSpec((1, d), lambda i, j: (0, 0)),
        pl.BlockSpec((1, d), lambda i, j: (0, 0)),
        pl.BlockSpec((d, tf), lambda i, j: (0, j)),
        pl.BlockSpec((d, tf), lambda i, j: (0, j + nf)),
        pl.BlockSpec((3, tf), lambda i, j: (0, j)),
        pl.BlockSpec((1, tf), lambda i, j: (0, j)),
        pl.BlockSpec((tf, d), lambda i, j: (j, 0)),
    ]


def _ffn_prompt(x, seq_len, g2, g3, w_up, conv_w, conv_b, w_down, *, tm=512, tf=512):
    n, d = x.shape
    dff = w_down.shape[0]
    nt = n // tm
    hr = BF16_ROWS
    in_specs = [
        pl.BlockSpec((tm, d), lambda i, j: (i, 0)),
        pl.BlockSpec((hr, d), lambda i, j: (jnp.maximum(i * (tm // hr) - 1, 0), 0)),
    ] + _ffn_common_specs(d, dff, tf)
    y, tail = pl.pallas_call(
        functools.partial(_ffn_prompt_kernel, tiles_per_seq=seq_len // tm),
        out_shape=[jax.ShapeDtypeStruct((n, d), F32), jax.ShapeDtypeStruct((nt * 8, dff), F32)],
        grid=(nt, dff // tf),
        in_specs=in_specs,
        out_specs=[pl.BlockSpec((tm, d), lambda i, j: (i, 0)), pl.BlockSpec((8, tf), lambda i, j: (i, j))],
        scratch_shapes=[pltpu.VMEM((tm + hr, d), BF16), pltpu.VMEM((tm, d), F32)],
        compiler_params=_cparams("parallel", "arbitrary"),
        name="ffn_prompt",
    )(x, x, g2.reshape(1, d), g3.reshape(1, d), w_up, w_up, conv_w, conv_b.reshape(1, dff), w_down)
    return y, tail


def _ffn_sample(x, seq, p1, p2, g2, g3, w_up, conv_w, conv_b, w_down, *, tm=512, tf=512):
    n, d = x.shape
    dff = w_down.shape[0]
    in_specs = [
        pl.BlockSpec((tm, d), lambda i, j: (i, 0)),
        pl.BlockSpec((tm, tf), lambda i, j: (i, j)),
        pl.BlockSpec((tm, tf), lambda i, j: (i, j)),
    ] + _ffn_common_specs(d, dff, tf)
    y, a = pl.pallas_call(
        functools.partial(_ffn_sample_kernel, seq=seq),
        out_shape=[jax.ShapeDtypeStruct((n, d), F32), jax.ShapeDtypeStruct((n, dff), F32)],
        grid=(n // tm, dff // tf),
        in_specs=in_specs,
        out_specs=[pl.BlockSpec((tm, d), lambda i, j: (i, 0)), pl.BlockSpec((tm, tf), lambda i, j: (i, j))],
        scratch_shapes=[pltpu.VMEM((tm, d), BF16), pltpu.VMEM((tm, d), F32)],
        compiler_params=_cparams("parallel", "arbitrary"),
        name="ffn_sample",
    )(x, p1, p2, g2.reshape(1, d), g3.reshape(1, d), w_up, w_up, conv_w, conv_b.reshape(1, dff), w_down)
    return y, a


def _attn_prompt_kernel(q_ref, kp_ref, kc_ref, vp_ref, vc_ref, o_ref, l_ref):
    i = pl.program_id(2)
    tq = q_ref.shape[0]
    qi = lax.broadcasted_iota(jnp.int32, (tq, 2 * tq), 0) + tq
    ki = lax.broadcasted_iota(jnp.int32, (tq, 2 * tq), 1)
    dist = qi - ki
    first_key = jnp.where(i > 0, 0, tq)
    valid = (dist >= 0) & (dist <= A_STEPS) & (ki >= first_key)
    scale = A_HEAD_DIM ** -0.5
    low = lax.broadcasted_iota(jnp.int32, (tq, LANES), 1) < A_HEAD_DIM
    for pair in range(A_WIDTH // LANES):
        cs = slice(pair * LANES, (pair + 1) * LANES)
        q_pair = q_ref[:, cs]
        k = jnp.concatenate([kp_ref[:, cs], kc_ref[:, cs]], axis=0).astype(BF16)
        v = jnp.concatenate([vp_ref[:, cs], vc_ref[:, cs]], axis=0).astype(BF16)
        outs, lses = [], []
        for q in (jnp.where(low, q_pair, 0.0), jnp.where(low, 0.0, q_pair)):
            s = jnp.where(valid, _dot_nt(q.astype(BF16), k) * scale, NEG_INF)
            m = jnp.max(s, axis=-1, keepdims=True)
            p = jnp.exp(s - m)
            l = jnp.sum(p, axis=-1, keepdims=True)
            outs.append(_dot(p.astype(BF16), v) / l)
            lses.append(m + jnp.log(l))
        o_ref[:, cs] = jnp.where(low, outs[0], outs[1])
        l_ref[:, cs] = jnp.where(low, lses[0], lses[1])


def _attn_prompt(qkv, batch, seq):
    dil, n = qkv.shape[1], qkv.shape[2]
    tq = A_STEPS

    def spec(role, prev):
        def imap(b, c, i):
            return (b, c, jnp.maximum(i - 1, 0) if prev else i, role)
        return pl.BlockSpec((None, None, tq, A_WIDTH), imap)

    out_spec = pl.BlockSpec((None, None, tq, A_WIDTH), lambda b, c, i: (b, c, i, 0))
    return pl.pallas_call(
        _attn_prompt_kernel,
        out_shape=[jax.ShapeDtypeStruct((batch, dil, n, A_WIDTH), F32)] * 2,
        grid=(batch, dil, n // tq),
        in_specs=[spec(0, False), spec(1, True), spec(1, False), spec(2, True), spec(2, False)],
        out_specs=[out_spec, out_spec],
        compiler_params=_cparams("parallel", "parallel", "arbitrary"),
        name="attn_prompt_d%d" % dil,
    )(qkv, qkv, qkv, qkv, qkv)


def _attn_sample_kernel(new_ref, c0_ref, c1_ref, c2_ref, o_ref):
    nrow = new_ref.shape[2]
    s_new = nrow // A_HEADS
    scale = A_HEAD_DIM ** -0.5
    row_n = lax.broadcasted_iota(jnp.int32, (nrow, nrow), 0)
    col_n = lax.broadcasted_iota(jnp.int32, (nrow, nrow), 1)
    age_n = (row_n & (s_new - 1)) - (col_n & (s_new - 1))
    s_shift = s_new.bit_length() - 1
    same_head_n = (row_n >> s_shift) == (col_n >> s_shift)

    crefs = (c0_ref, c1_ref, c2_ref)
    chains = [(g, h) for g in range(len(A_GROUPS)) for h in range(A_HEADS)]
    qfs, vns, s_owns, valids = [], [], [], []
    for g, ((_, dil), cref) in enumerate(zip(A_GROUPS, crefs)):
        n_past = cref.shape[3]
        qfs.append(new_ref[g, 0])
        vns.append(new_ref[g, 2].astype(BF16))
        valid_n = same_head_n & (age_n >= 0) & ((age_n & (dil - 1)) == 0)
        s_owns.append(jnp.where(valid_n, _dot_nt(qfs[g].astype(BF16), new_ref[g, 1].astype(BF16)) * scale, NEG_INF))
        age = (lax.broadcasted_iota(jnp.int32, (s_new, n_past), 0)
               - lax.broadcasted_iota(jnp.int32, (s_new, n_past), 1))
        valids.append((age <= 0) & ((age & (dil - 1)) == 0))

    def rows(h):
        return slice(h * s_new, (h + 1) * s_new)

    s_bufs = [jnp.where(valids[g], _dot(qfs[g][rows(h), :].astype(BF16), crefs[g][0, h].astype(BF16)) * scale,
                        NEG_INF) for g, h in chains]
    ms = [jnp.maximum(jnp.max(s_buf, axis=-1, keepdims=True), jnp.max(s_owns[g][rows(h), :], axis=-1, keepdims=True))
          for (g, h), s_buf in zip(chains, s_bufs)]
    p_bufs = [jnp.exp(s_buf - m) for s_buf, m in zip(s_bufs, ms)]
    p_owns = [jnp.exp(s_owns[g][rows(h), :] - m) for (g, h), m in zip(chains, ms)]
    ls = [jnp.sum(pb, axis=-1, keepdims=True) + jnp.sum(po, axis=-1, keepdims=True) for pb, po in zip(p_bufs, p_owns)]
    os = [_dot_nt(pb.astype(BF16), crefs[g][1, h].astype(BF16)) + _dot(po.astype(BF16), vns[g])
          for (g, h), pb, po in zip(chains, p_bufs, p_owns)]
    outs, lses = [], []
    for g in range(len(A_GROUPS)):
        idx = [i for i, (gg, _) in enumerate(chains) if gg == g]
        outs.append(jnp.concatenate([os[i] / ls[i] for i in idx], axis=0))
        lses.append(jnp.concatenate([ms[i] + jnp.log(ls[i]) for i in idx], axis=0))

    mx = jnp.maximum(jnp.maximum(lses[0], lses[1]), lses[2])
    es = [jnp.exp(x - mx) for x in lses]
    den = es[0] + es[1] + es[2]
    o_ref[...] = (es[0] / den) * outs[0] + (es[1] / den) * outs[1] + (es[2] / den) * outs[2]


def _attn_sample(qkv, caches, j, batch, s_new):
    new = qkv.reshape(batch, s_new, len(A_GROUPS), 3, A_HEADS, A_HEAD_DIM)
    new = new.transpose(0, 2, 3, 4, 1, 5).reshape(batch, len(A_GROUPS), 3, A_HEADS * s_new, A_HEAD_DIM)
    views = []
    specs = [pl.BlockSpec((None,) + new.shape[1:], lambda b: (b, 0, 0, 0, 0))]
    for (win, dil), cache in zip(A_GROUPS, caches):
        n_past = cache.shape[2]
        assert n_past == dil * A_STEPS
        views.append(cache.transpose(0, 1, 3, 4, 5, 2))
        specs.append(pl.BlockSpec((None, None, 2, A_HEADS, A_HEAD_DIM, n_past), lambda b: (j, b, 0, 0, 0, 0)))
    out = pl.pallas_call(
        _attn_sample_kernel,
        out_shape=jax.ShapeDtypeStruct((batch, A_HEADS * s_new, A_HEAD_DIM), F32),
        grid=(batch,),
        in_specs=specs,
        out_specs=pl.BlockSpec((None, A_HEADS * s_new, A_HEAD_DIM), lambda b: (b, 0, 0)),
        compiler_params=_cparams("parallel"),
        name="attn_sample",
    )(new, *views)
    out = out.reshape(batch, A_HEADS, s_new, A_HEAD_DIM).transpose(0, 2, 1, 3)
    return out.reshape(batch * s_new, A_WIDTH)


def _retention_kernel(q_ref, k_ref, v_ref, g_ref, s0_ref, dec_ref, inn_ref, tail_ref, cd_ref,
                      o_ref, sout_ref, s_scr):
    c = pl.program_id(1)
    lk = dec_ref.shape[2]

    @pl.when(c == 0)
    def _():
        s_scr[...] = s0_ref[...]

    for h in range(R_HEADS):
        ks = slice(h * R_DK, (h + 1) * R_DK)
        vs = slice(h * R_DV, (h + 1) * R_DV)
        q = q_ref[:, ks].astype(BF16)
        kf = _pad_rows(k_ref[:, ks] * (R_DK ** -0.5), lk)
        vb = _pad_rows(v_ref[:, vs], lk).astype(BF16)
        s = _dot_nt(q, kf.astype(BF16)) * dec_ref[h]
        state = s_scr[h]
        o = _dot(s.astype(BF16), vb) + _dot(q, state.astype(BF16)) * inn_ref[h]
        s_scr[h] = state * cd_ref[h] + _dot_tn((kf * tail_ref[h]).astype(BF16), vb)
        o_ref[:, vs] = (jax.nn.silu(g_ref[:, vs]) * _rms(o)).astype(o_ref.dtype)

    @pl.when(c == pl.num_programs(1) - 1)
    def _():
        sout_ref[...] = s_scr[...]


def _retention(proj, state, batch, seq):
    chunk = CHUNK if seq % CHUNK == 0 else seq
    lk = max(chunk, LANES)
    nc = seq // chunk
    nq, nv = R_HEADS * R_DK, R_HEADS * R_DV
    log_g = jnp.log1p(-jnp.exp2(-5.0 - jnp.arange(R_HEADS, dtype=F32)))
    i = jnp.arange(chunk, dtype=F32)
    diff = i[:, None] - i[None, :]
    decay = jnp.where(diff >= 0, jnp.exp(jnp.maximum(diff, 0.0)[None] * log_g[:, None, None]), 0.0)
    decay = jnp.pad(decay, ((0, 0), (0, 0), (0, lk - chunk)))
    inner = jnp.exp((i + 1.0)[None, :] * log_g[:, None])[:, :, None]
    tail = jnp.exp((chunk - 1.0 - i)[None, :] * log_g[:, None])
    tail = jnp.pad(tail, ((0, 0), (0, lk - chunk)))[:, :, None]
    cdec = jnp.exp(chunk * log_g)[:, None, None]
    out_dtype = BF16 if chunk % BF16_ROWS == 0 else F32
    whole = lambda b, c: (0, 0, 0)
    gated, new_state = pl.pallas_call(
        _retention_kernel,
        out_shape=[jax.ShapeDtypeStruct((batch * seq, nv), out_dtype),
                   jax.ShapeDtypeStruct(state.shape, F32)],
        grid=(batch, nc),
        in_specs=[
            pl.BlockSpec((chunk, nq), lambda b, c: (b * nc + c, 0)),
            pl.BlockSpec((chunk, nq), lambda b, c: (b * nc + c, 1)),
            pl.BlockSpec((chunk, nv), lambda b, c: (b * nc + c, 1)),
            pl.BlockSpec((chunk, nv), lambda b, c: (b * nc + c, 2)),
            pl.BlockSpec((None, R_HEADS, R_DK, R_DV), lambda b, c: (b, 0, 0, 0)),
            pl.BlockSpec((R_HEADS, chunk, lk), whole),
            pl.BlockSpec((R_HEADS, chunk, 1), whole),
            pl.BlockSpec((R_HEADS, lk, 1), whole),
            pl.BlockSpec((R_HEADS, 1, 1), whole),
        ],
        out_specs=[pl.BlockSpec((chunk, nv), lambda b, c: (b * nc + c, 0)),
                   pl.BlockSpec((None, R_HEADS, R_DK, R_DV), lambda b, c: (b, 0, 0, 0))],
        scratch_shapes=[pltpu.VMEM((R_HEADS, R_DK, R_DV), F32)],
        compiler_params=_cparams("parallel", "arbitrary"),
        name="retention",
    )(proj, proj, proj, proj, state, decay, inner, tail, cdec)
    return gated, new_state


def _mlstm_kernel(q_ref, k_ref, v_ref, og_ref, gt_ref, bias_ref, ng_ref, c0_ref, n0_ref, m0_ref,
                  o_ref, cout_ref, nout_ref, mout_ref, c_scr, n_scr, m_scr):
    c = pl.program_id(1)
    chunk = q_ref.shape[0]
    lk = max(chunk, LANES)

    @pl.when(c == 0)
    def _():
        c_scr[...] = c0_ref[...]
        n_scr[...] = n0_ref[...]
        m_scr[...] = m0_ref[...]

    pre = gt_ref[...] + bias_ref[...]
    log_f = -(jnp.maximum(-pre, 0.0) + jnp.log1p(jnp.exp(-jnp.abs(pre))))
    rowi = lax.broadcasted_iota(jnp.int32, pre.shape, 0)
    lane = lax.broadcasted_iota(jnp.int32, pre.shape, 1)
    cum = log_f
    d = 1
    while d < chunk:
        cum = cum + jnp.where(rowi >= d, pltpu.roll(cum, d, 0), 0.0)
        d *= 2
    cols = jnp.where(lane < M_HEADS, pre, cum)
    rows = _pad_rows(cols, lk).T
    t_idx = lax.broadcasted_iota(jnp.int32, (chunk, lk), 0)
    s_idx = lax.broadcasted_iota(jnp.int32, (chunk, lk), 1)
    causal = s_idx <= t_idx

    for h in range(M_HEADS):
        ks = slice(h * M_DK, (h + 1) * M_DK)
        vs = slice(h * M_DV, (h + 1) * M_DV)
        i_row = rows[h:h + 1, :]
        b_row = rows[M_HEADS + h:M_HEADS + h + 1, :]
        i_col = cols[:, h:h + 1]
        b_col = cols[:, M_HEADS + h:M_HEADS + h + 1]
        m_prev = m_scr[h:h + 1, 0:1]
        logw = jnp.where(causal, (b_col - b_row) + i_row, NEG_INF)
        inter = b_col + m_prev
        m_t = jnp.maximum(inter, jnp.max(logw, axis=-1, keepdims=True))
        w = jnp.exp(logw - m_t)
        a = jnp.exp(inter - m_t)

        qf = q_ref[:, ks]
        q = qf.astype(BF16)
        kf = k_ref[:, ks] * (M_DK ** -0.5)
        vb = _pad_rows(v_ref[:, vs], lk).astype(BF16)
        qk = _dot_nt(q, _pad_rows(kf, lk).astype(BF16)) * w
        cmat = c_scr[h]
        nrow = n_scr[h:h + 1, :]
        num = _dot(qk.astype(BF16), vb) + a * _dot(q, cmat.astype(BF16))
        den = jnp.sum(qk, axis=-1, keepdims=True) + a * jnp.sum(qf * nrow, axis=-1, keepdims=True)
        hid = num / jnp.maximum(jnp.abs(den), jnp.exp(-m_t))

        b_end = b_col[chunk - 1:chunk, :]
        m_end = m_t[chunk - 1:chunk, :]
        a_end = a[chunk - 1:chunk, :]
        kw = kf * jnp.exp(((b_end - b_col) + i_col) - m_end)
        c_scr[h] = a_end * cmat + _dot_tn(_pad_rows(kw, lk).astype(BF16), vb)
        n_scr[h:h + 1, :] = a_end * nrow + jnp.sum(kw, axis=0, keepdims=True)
        m_scr[h:h + 1, :] = jnp.broadcast_to(m_end, (1, m_scr.shape[1]))

        y = _rms(hid) * ng_ref[h:h + 1, :]
        o_ref[:, vs] = (jax.nn.sigmoid(og_ref[:, vs]) * y).astype(o_ref.dtype)

    @pl.when(c == pl.num_programs(1) - 1)
    def _():
        cout_ref[...] = c_scr[...]
        nout_ref[...] = n_scr[...]
        mout_ref[...] = m_scr[...]


def _mlstm(proj, gates, bias_row, norm_g, c0, n0, m0, batch, seq):
    chunk = CHUNK if seq % CHUNK == 0 else seq
    nc = seq // chunk
    nq, nv = M_HEADS * M_DK, M_HEADS * M_DV
    m0b = jnp.broadcast_to(m0[:, :, None], (batch, M_HEADS, LANES))
    out_dtype = BF16 if chunk % BF16_ROWS == 0 else F32
    st4 = lambda b, c: (b, 0, 0, 0)
    st3 = lambda b, c: (b, 0, 0)
    gated, c_new, n_new, m_new = pl.pallas_call(
        _mlstm_kernel,
        out_shape=[jax.ShapeDtypeStruct((batch * seq, nv), out_dtype),
                   jax.ShapeDtypeStruct(c0.shape, F32),
                   jax.ShapeDtypeStruct(n0.shape, F32),
                   jax.ShapeDtypeStruct(m0b.shape, F32)],
        grid=(batch, nc),
        in_specs=[
            pl.BlockSpec((chunk, nq), lambda b, c: (b * nc + c, 0)),
            pl.BlockSpec((chunk, nq), lambda b, c: (b * nc + c, 1)),
            pl.BlockSpec((chunk, nv), lambda b, c: (b * nc + c, 1)),
            pl.BlockSpec((chunk, nv), lambda b, c: (b * nc + c, 2)),
            pl.BlockSpec((chunk, LANES), lambda b, c: (b * nc + c, 0)),
            pl.BlockSpec((1, LANES), lambda b, c: (0, 0)),
            pl.BlockSpec((M_HEADS, M_DV), lambda b, c: (0, 0)),
            pl.BlockSpec((None, M_HEADS, M_DK, M_DV), st4),
            pl.BlockSpec((None, M_HEADS, M_DK), st3),
            pl.BlockSpec((None, M_HEADS, LANES), st3),
        ],
        out_specs=[pl.BlockSpec((chunk, nv), lambda b, c: (b * nc + c, 0)),
                   pl.BlockSpec((None, M_HEADS, M_DK, M_DV), st4),
                   pl.BlockSpec((None, M_HEADS, M_DK), st3),
                   pl.BlockSpec((None, M_HEADS, LANES), st3)],
        scratch_shapes=[pltpu.VMEM((M_HEADS, M_DK, M_DV), F32),
                        pltpu.VMEM((M_HEADS, M_DK), F32),
                        pltpu.VMEM((M_HEADS, LANES), F32)],
        compiler_params=_cparams("parallel", "arbitrary"),
        name="mlstm",
    )(proj, proj, proj, proj, gates, bias_row, norm_g, c0, n0, m0b)
    return gated, c_new, n_new, m_new[:, :, 0]


def _rope_tables_a(pos):
    inv = ROPE_THETA ** (-jnp.arange(ROPE_DIM // 2, dtype=F32) * (2.0 / ROPE_DIM))
    ang = pos.astype(F32)[:, None] * inv[None, :]
    cos, sin = jnp.cos(ang), jnp.sin(ang)
    half = ROPE_DIM // 2
    n = pos.shape[0]
    pad = A_HEAD_DIM - ROPE_DIM
    c = jnp.concatenate([cos, cos, jnp.ones((n, pad), F32)], axis=1)
    sp = jnp.concatenate([jnp.zeros((n, half), F32), sin, jnp.zeros((n, pad), F32)], axis=1)
    sm = jnp.concatenate([-sin, jnp.zeros((n, half + pad), F32)], axis=1)
    reps = LANES // A_HEAD_DIM
    return tuple(jnp.tile(t, (1, reps)) for t in (c, sp, sm))


def _rope_tables_b(pos):
    inv = R_THETA ** (-jnp.linspace(0.0, 1.0, R_DK // 2, dtype=F32))
    ang = pos.astype(F32)[:, None] * inv[None, :]
    return jnp.cos(ang), jnp.sin(ang)


def kernel(x_prompt, x_sample, cache_win128_kv, cache_win512_kv, cache_win2048_kv, state_ret,
           state_mlstm_c, state_mlstm_n, state_mlstm_m, state_ffn_conv, norm_gains,
           a_w_in, a_w_out, b_w_in, b_w_out, c_w_in, c_b_if, c_norm, c_w_out,
           f_w_up, f_conv_w, f_conv_b, f_w_down):
    bp, t, d = x_prompt.shape
    bs, s, _ = x_sample.shape
    depth = norm_gains.shape[0]
    caches = (cache_win128_kv, cache_win512_kv, cache_win2048_kv)
    pos_p = jnp.arange(t, dtype=jnp.int32)
    pos_s = jnp.tile(PAST_LEN + jnp.arange(s, dtype=jnp.int32), bs)
    tabs_a_p, tabs_a_s = _rope_tables_a(pos_p), _rope_tables_a(pos_s)
    tabs_b_p, tabs_b_s = _rope_tables_b(pos_p), _rope_tables_b(pos_s)
    tm = 512
    tm_p = 1024
    tm_f, tf_f = 1024, 512

    xp = x_prompt.reshape(bp * t, d)
    xs = x_sample.reshape(bs * s, d)
    win_p = [[] for _ in A_GROUPS]
    win_s = [[] for _ in A_GROUPS]
    ret_p, ret_s = [], []
    c_p, c_s, n_p, n_s, m_p, m_s = [], [], [], [], [], []
    conv_p, conv_s = [], []

    for layer in range(depth):
        kind, j = layer % N_MIXERS, layer // N_MIXERS
        g = norm_gains[layer]
        if kind == 0:
            w_in = a_w_in[j].astype(BF16)
            w_out = a_w_out[j].astype(BF16)
            qkv_p = _proj_a_prompt(xp, g[0], w_in, tabs_a_p, bp, t, tm=tm_p)
            qkv_s = _norm_proj(xs, g[0], w_in, mode="rope_a", tabs=tabs_a_s, tab_blocks=bs * s // tm, tm=tm)
            outs, lses = [], []
            for gi, (win, dil) in enumerate(A_GROUPS):
                o, l = _attn_prompt(qkv_p[gi], bp, t)
                outs.append(o)
                lses.append(l)
                keep = min(win, t)
                kv = qkv_p[gi][:, :, (t - keep) // dil:, A_WIDTH:].transpose(0, 2, 1, 3)
                win_p[gi].append(kv.reshape(bp, keep, 2, A_HEADS, A_HEAD_DIM))
                lo = gi * 3 * A_WIDTH + A_WIDTH
                kv = qkv_s.reshape(bs, s, -1)[:, :, lo:lo + 2 * A_WIDTH]
                win_s[gi].append(kv.reshape(bs, s, 2, A_HEADS, A_HEAD_DIM))
            xp = _post_merge(outs, lses, w_out, xp, g[1], t, tm=tm)
            att_s = _attn_sample(qkv_s, caches, j, bs, s)
            xs = _post(att_s, w_out, xs, g[1], tm=tm)
        elif kind == 1:
            w_in = b_w_in[j].astype(BF16)
            w_out = b_w_out[j].astype(BF16)
            proj_p = _norm_proj(xp, g[0], w_in, mode="rope_b", tabs=tabs_b_p, tab_blocks=t // tm_p, tm=tm_p)
            proj_s = _norm_proj(xs, g[0], w_in, mode="rope_b", tabs=tabs_b_s, tab_blocks=bs * s // tm, tm=tm)
            gated_p, sp_new = _retention(proj_p, jnp.zeros((bp, R_HEADS, R_DK, R_DV), F32), bp, t)
            gated_s, ss_new = _retention(proj_s, state_ret[j], bs, s)
            ret_p.append(sp_new)
            ret_s.append(ss_new)
            xp = _post(gated_p, w_out, xp, g[1], tm=tm)
            xs = _post(gated_s, w_out, xs, g[1], tm=tm)
        else:
            n_main = 2 * M_HEADS * M_DK + 2 * M_HEADS * M_DV
            w_in = c_w_in[j].astype(BF16)
            w_g = jnp.pad(c_w_in[j][:, n_main:], ((0, 0), (0, LANES - 2 * M_HEADS))).astype(BF16)
            w_out = c_w_out[j].astype(BF16)
            bias_row = jnp.pad(c_b_if[j].reshape(1, 2 * M_HEADS), ((0, 0), (0, LANES - 2 * M_HEADS)))
            proj_p, gates_p = _norm_proj(xp, g[0], w_in, mode="gates", wg=w_g, tm=tm_p)
            proj_s, gates_s = _norm_proj(xs, g[0], w_in, mode="gates", wg=w_g, tm=tm)
            zc = jnp.zeros((bp, M_HEADS, M_DK, M_DV), F32)
            zn = jnp.zeros((bp, M_HEADS, M_DK), F32)
            zm = jnp.zeros((bp, M_HEADS), F32)
            gated_p, cp, np_, mp = _mlstm(proj_p, gates_p, bias_row, c_norm[j], zc, zn, zm, bp, t)
            gated_s, cs, ns, ms = _mlstm(proj_s, gates_s, bias_row, c_norm[j], state_mlstm_c[j],
                                         state_mlstm_n[j], state_mlstm_m[j], bs, s)
            c_p.append(cp); c_s.append(cs)
            n_p.append(np_); n_s.append(ns)
            m_p.append(mp); m_s.append(ms)
            xp = _post(gated_p, w_out, xp, g[1], tm=tm)
            xs = _post(gated_s, w_out, xs, g[1], tm=tm)

        w_up = f_w_up[layer].astype(BF16)
        w_down = f_w_down[layer].astype(BF16)
        xp, tail = _ffn_prompt(xp, t, g[2], g[3], w_up, f_conv_w[layer], f_conv_b[layer], w_down, tm=tm_f, tf=tf_f)
        tail = tail.reshape(bp, t // tm_f, 8, D_FF)
        conv_p.append(tail[:, -1, 6:, :])
        buf = state_ffn_conv[layer]
        zrow = jnp.zeros((bs, s - 1, D_FF), F32)
        p1 = jnp.concatenate([buf[:, 1:2], zrow], axis=1).reshape(bs * s, D_FF)
        p2 = jnp.concatenate([buf, zrow[:, 1:]], axis=1).reshape(bs * s, D_FF)
        xs, a_s = _ffn_sample(xs, s, p1, p2, g[2], g[3], w_up, f_conv_w[layer], f_conv_b[layer], w_down, tm=tm)
        conv_s.append(a_s.reshape(bs, s, D_FF)[:, s - 2:, :])

    return (xp.reshape(bp, t, d), xs.reshape(bs, s, d),
            jnp.stack(win_p[0]), jnp.stack(win_s[0]),
            jnp.stack(win_p[1]), jnp.stack(win_s[1]),
            jnp.stack(win_p[2]), jnp.stack(win_s[2]),
            jnp.stack(ret_p), jnp.stack(ret_s),
            jnp.stack(c_p), jnp.stack(c_s),
            jnp.stack(n_p), jnp.stack(n_s),
            jnp.stack(m_p), jnp.stack(m_s),
            jnp.stack(conv_p), jnp.stack(conv_s))
```

```python
import functools

import jax
import jax.numpy as jnp
from jax import lax
from jax.experimental import pallas as pl
from jax.experimental.pallas import tpu as pltpu

F32 = jnp.float32
BF16 = jnp.bfloat16

D_MODEL = 1024
PAST_LEN = 2048
N_MIXERS = 3
A_GROUPS = ((128, 1), (512, 4), (2048, 16))
A_HEADS = 8
A_HEAD_DIM = 64
A_WIDTH = A_HEADS * A_HEAD_DIM
A_STEPS = 128
ROPE_THETA = 500000.0
ROPE_DIM = A_HEAD_DIM // 4
R_HEADS = 4
R_DK = 256
R_DV = 512
R_THETA = 10000.0
M_HEADS = 4
M_DK = 256
M_DV = 512
CHUNK = 128
D_FF = 4 * D_MODEL
EPS = 1e-6

LANES = 128
BF16_ROWS = 16
VMEM_LIMIT = 52 * 1024 * 1024

NT_DIMS = (((1,), (1,)), ((), ()))
TN_DIMS = (((0,), (0,)), ((), ()))
NEG_INF = float("-inf")


def _cparams(*sem):
    return pltpu.CompilerParams(dimension_semantics=sem, vmem_limit_bytes=VMEM_LIMIT)


def _rms(x):
    return x * lax.rsqrt(jnp.mean(x * x, axis=-1, keepdims=True) + EPS)


def _dot(a, b):
    return jnp.dot(a, b, preferred_element_type=F32)


def _dot_nt(a, b):
    return lax.dot_general(a, b, NT_DIMS, preferred_element_type=F32)


def _dot_tn(a, b):
    return lax.dot_general(a, b, TN_DIMS, preferred_element_type=F32)


def _pad_rows(x, rows):
    if x.shape[0] == rows:
        return x
    return jnp.concatenate([x, jnp.zeros((rows - x.shape[0], x.shape[1]), x.dtype)], axis=0)


def _norm_proj_kernel(*refs, mode):
    if mode == "rope_a":
        x_ref, g_ref, w_ref, c_ref, sp_ref, sm_ref, o_ref, h_scr = refs
    elif mode == "rope_b":
        x_ref, g_ref, w_ref, c_ref, s_ref, o_ref, h_scr = refs
    else:
        x_ref, g_ref, w_ref, wg_ref, o_ref, og_ref, h_scr = refs
    j = pl.program_id(1)

    @pl.when(j == 0)
    def _():
        h_scr[...] = (_rms(x_ref[...]) * g_ref[...]).astype(BF16)
        if mode == "gates":
            og_ref[...] = _dot(h_scr[...], wg_ref[...])

    acc = _dot(h_scr[...], w_ref[...])
    if mode == "rope_a":
        @pl.when(j % 3 != 2)
        def _():
            o_ref[...] = _rope_a(acc, c_ref, sp_ref, sm_ref)

        @pl.when(j % 3 == 2)
        def _():
            o_ref[...] = acc
    elif mode == "rope_b":
        @pl.when(j < 4)
        def _():
            c = c_ref[...]
            s = s_ref[...]
            outs = []
            for hh in range(acc.shape[1] // R_DK):
                x1 = acc[:, hh * R_DK: hh * R_DK + R_DK // 2]
                x2 = acc[:, hh * R_DK + R_DK // 2: (hh + 1) * R_DK]
                outs += [x1 * c - x2 * s, x2 * c + x1 * s]
            o_ref[...] = jnp.concatenate(outs, axis=1)

        @pl.when(j >= 4)
        def _():
            o_ref[...] = acc
    else:
        o_ref[...] = acc


def _norm_proj(x, gain, w, *, mode, tabs=(), tab_blocks=1, wg=None, tm=512, tn=512):
    n, d = x.shape
    m = (w.shape[1] // tn) * tn
    grid = (n // tm, m // tn)
    in_specs = [
        pl.BlockSpec((tm, d), lambda i, j: (i, 0)),
        pl.BlockSpec((1, d), lambda i, j: (0, 0)),
        pl.BlockSpec((d, tn), lambda i, j: (0, j)),
    ]
    args = [x, gain.reshape(1, d), w]
    out_shape = [jax.ShapeDtypeStruct((n, m), F32)]
    out_specs = [pl.BlockSpec((tm, tn), lambda i, j: (i, j))]
    if mode in ("rope_a", "rope_b"):
        for t in tabs:
            in_specs.append(pl.BlockSpec((tm, LANES), lambda i, j: (i % tab_blocks, 0)))
            args.append(t)
    else:
        in_specs.append(pl.BlockSpec((d, LANES), lambda i, j: (0, 0)))
        args.append(wg)
        out_shape.append(jax.ShapeDtypeStruct((n, LANES), F32))
        out_specs.append(pl.BlockSpec((tm, LANES), lambda i, j: (i, 0)))
    outs = pl.pallas_call(
        functools.partial(_norm_proj_kernel, mode=mode),
        out_shape=out_shape,
        grid=grid,
        in_specs=in_specs,
        out_specs=out_specs,
        scratch_shapes=[pltpu.VMEM((tm, d), BF16)],
        compiler_params=_cparams("parallel", "arbitrary"),
        name="norm_proj_" + mode,
    )(*args)
    return outs if mode == "gates" else outs[0]


def _rope_a(acc, c_ref, sp_ref, sm_ref):
    c, sp, sm = c_ref[...], sp_ref[...], sm_ref[...]
    half = ROPE_DIM // 2
    cols = []
    for k in range(acc.shape[1] // LANES):
        a = acc[:, k * LANES:(k + 1) * LANES]
        up = pltpu.roll(a, half, 1)
        down = pltpu.roll(a, LANES - half, 1)
        cols.append(a * c + down * sm + up * sp)
    return jnp.concatenate(cols, axis=1)


def _proj_a_prompt_kernel(x_ref, g_ref, w_ref, c_ref, sp_ref, sm_ref, o0_ref, o1_ref, o2_ref, h_scr, acc_scr):
    j = pl.program_id(1)
    tm = x_ref.shape[0]
    chunks = A_WIDTH // LANES

    @pl.when(j == 0)
    def _():
        h_scr[...] = (_rms(x_ref[...]) * g_ref[...]).astype(BF16)

    acc = _dot(h_scr[...], w_ref[...])

    def stash(val):
        for k in range(chunks):
            acc_scr[k] = val[:, k * LANES:(k + 1) * LANES]

    @pl.when(j % 3 != 2)
    def _():
        stash(_rope_a(acc, c_ref, sp_ref, sm_ref))

    @pl.when(j % 3 == 2)
    def _():
        stash(acc)

    for g, o_ref in enumerate((o0_ref, o1_ref, o2_ref)):
        dil = o_ref.shape[0]
        rows = tm // dil

        @pl.when(j // 3 == g)
        def _(o_ref=o_ref, dil=dil, rows=rows):
            for c in range(dil):
                o_ref[c] = jnp.concatenate(
                    [acc_scr[k, pl.ds(c, rows, stride=dil), :] for k in range(chunks)], axis=1)


def _proj_a_prompt(x, gain, w, tabs, batch, seq, *, tm=1024):
    n, d = x.shape
    per_seq = seq // tm
    out_shape, out_specs = [], []
    for g, (_, dil) in enumerate(A_GROUPS):
        out_shape.append(jax.ShapeDtypeStruct((batch, dil, seq // dil, 3 * A_WIDTH), F32))
        out_specs.append(pl.BlockSpec(
            (None, dil, tm // dil, A_WIDTH),
            lambda i, j, g=g: (i // per_seq, 0, i % per_seq, jnp.clip(j - 3 * g, 0, 2))))
    in_specs = [
        pl.BlockSpec((tm, d), lambda i, j: (i, 0)),
        pl.BlockSpec((1, d), lambda i, j: (0, 0)),
        pl.BlockSpec((d, A_WIDTH), lambda i, j: (0, j)),
    ] + [pl.BlockSpec((tm, LANES), lambda i, j: (i % per_seq, 0)) for _ in tabs]
    return pl.pallas_call(
        _proj_a_prompt_kernel,
        out_shape=out_shape,
        grid=(n // tm, w.shape[1] // A_WIDTH),
        in_specs=in_specs,
        out_specs=out_specs,
        scratch_shapes=[pltpu.VMEM((tm, d), BF16), pltpu.VMEM((A_WIDTH // LANES, tm, LANES), F32)],
        compiler_params=_cparams("parallel", "arbitrary"),
        name="proj_a_prompt",
    )(x, gain.reshape(1, d), w, *tabs)


def _post_kernel(a_ref, w_ref, x_ref, g_ref, out_ref):
    y = _dot(a_ref[...].astype(BF16), w_ref[...])
    out_ref[...] = x_ref[...] + _rms(y) * g_ref[...]


def _post(a, w, x, gain, *, tm=512):
    n, d = x.shape
    k = w.shape[0]
    return pl.pallas_call(
        _post_kernel,
        out_shape=jax.ShapeDtypeStruct((n, d), F32),
        grid=(n // tm,),
        in_specs=[
            pl.BlockSpec((tm, k), lambda i: (i, 0)),
            pl.BlockSpec((k, d), lambda i: (0, 0)),
            pl.BlockSpec((tm, d), lambda i: (i, 0)),
            pl.BlockSpec((1, d), lambda i: (0, 0)),
        ],
        out_specs=pl.BlockSpec((tm, d), lambda i: (i, 0)),
        compiler_params=_cparams("parallel"),
        name="post",
    )(a, w, x, gain.reshape(1, d))


def _post_merge_kernel(o0, o1, o2, l0, l1, l2, w_ref, x_ref, g_ref, out_ref, scr):
    tm = x_ref.shape[0]
    chunks = A_WIDTH // LANES

    def token_order(ref, slot):
        dil = ref.shape[0]
        if dil == 1:
            return ref[0]
        rows = tm // dil
        for c in range(dil):
            v = ref[c]
            for k in range(chunks):
                scr[slot, k, pl.ds(c, rows, stride=dil), :] = v[:, k * LANES:(k + 1) * LANES]
        return jnp.concatenate([scr[slot, k] for k in range(chunks)], axis=1)

    la, lb, lc = token_order(l0, 0), token_order(l1, 0), token_order(l2, 1)
    mx = jnp.maximum(jnp.maximum(la, lb), lc)
    ea, eb, ec = jnp.exp(la - mx), jnp.exp(lb - mx), jnp.exp(lc - mx)
    den = ea + eb + ec
    a = (ea / den) * token_order(o0, 2) + (eb / den) * token_order(o1, 2) + (ec / den) * token_order(o2, 3)
    y = _dot(a.astype(BF16), w_ref[...])
    out_ref[...] = x_ref[...] + _rms(y) * g_ref[...]


def _post_merge(outs, lses, w, x, gain, seq, *, tm=512):
    n, d = x.shape
    k = w.shape[0]
    per_seq = seq // tm

    def cm_spec(a):
        dil = a.shape[1]
        return pl.BlockSpec((None, dil, tm // dil, k), lambda i: (i // per_seq, 0, i % per_seq, 0))

    return pl.pallas_call(
        _post_merge_kernel,
        out_shape=jax.ShapeDtypeStruct((n, d), F32),
        grid=(n // tm,),
        in_specs=[cm_spec(a) for a in outs + lses] + [
            pl.BlockSpec((k, d), lambda i: (0, 0)),
            pl.BlockSpec((tm, d), lambda i: (i, 0)),
            pl.BlockSpec((1, d), lambda i: (0, 0)),
        ],
        out_specs=pl.BlockSpec((tm, d), lambda i: (i, 0)),
        scratch_shapes=[pltpu.VMEM((4, k // LANES, tm, LANES), F32)],
        compiler_params=_cparams("parallel"),
        name="post_merge",
    )(*outs, *lses, w, x, gain.reshape(1, d))


def _ffn_tail(conv, b, wd_ref, acc_scr, x_ref, g3_ref, o_ref):
    j = pl.program_id(1)
    act = jax.nn.gelu(conv, approximate=True) * b
    acc_scr[...] += _dot(act.astype(BF16), wd_ref[...])

    @pl.when(j == pl.num_programs(1) - 1)
    def _():
        o_ref[...] = x_ref[...] + _rms(acc_scr[...]) * g3_ref[...]


def _ffn_prompt_kernel(x_ref, halo_ref, g2_ref, g3_ref, wa_ref, wb_ref, cw_ref, cb_ref, wd_ref,
                       o_ref, tail_ref, h_scr, acc_scr, *, tiles_per_seq):
    i = pl.program_id(0)
    j = pl.program_id(1)
    hr = halo_ref.shape[0]
    tm = x_ref.shape[0]

    @pl.when(j == 0)
    def _():
        keep = jnp.where(i % tiles_per_seq != 0, 1.0, 0.0)
        h_scr[:hr, :] = (_rms(halo_ref[...]) * g2_ref[...] * keep).astype(BF16)
        h_scr[hr:, :] = (_rms(x_ref[...]) * g2_ref[...]).astype(BF16)
        acc_scr[...] = jnp.zeros_like(acc_scr)

    a_ext = _dot(h_scr[...], wa_ref[...])
    b = _dot(h_scr[hr:, :], wb_ref[...])
    a = a_ext[hr:, :]
    prev1 = a_ext[hr - 1:hr, :]
    prev2 = a_ext[hr - 2:hr - 1, :]
    r1, r2 = pltpu.roll(a, 1, 0), pltpu.roll(a, 2, 0)
    row = lax.broadcasted_iota(jnp.int32, (8, a.shape[1]), 0)
    r1 = jnp.concatenate([jnp.where(row == 0, prev1, r1[:8]), r1[8:]], axis=0)
    r2 = jnp.concatenate([jnp.where(row == 0, prev2, jnp.where(row == 1, prev1, r2[:8])), r2[8:]], axis=0)
    cw = cw_ref[...]
    conv = cb_ref[...] + ((r2 * cw[0:1, :] + r1 * cw[1:2, :]) + a * cw[2:3, :])
    tail_ref[...] = a[tm - 8:, :]
    _ffn_tail(conv, b, wd_ref, acc_scr, x_ref, g3_ref, o_ref)


def _ffn_sample_kernel(x_ref, p1_ref, p2_ref, g2_ref, g3_ref, wa_ref, wb_ref, cw_ref, cb_ref, wd_ref,
                       o_ref, a_ref, h_scr, acc_scr, *, seq):
    j = pl.program_id(1)

    @pl.when(j == 0)
    def _():
        h_scr[...] = (_rms(x_ref[...]) * g2_ref[...]).astype(BF16)
        acc_scr[...] = jnp.zeros_like(acc_scr)

    a = _dot(h_scr[...], wa_ref[...])
    b = _dot(h_scr[...], wb_ref[...])
    pos = lax.broadcasted_iota(jnp.int32, a.shape, 0) & (seq - 1)
    r1 = jnp.where(pos == 0, p1_ref[...], pltpu.roll(a, 1, 0))
    r2 = jnp.where(pos < 2, p2_ref[...], pltpu.roll(a, 2, 0))
    cw = cw_ref[...]
    conv = cb_ref[...] + ((r2 * cw[0:1, :] + r1 * cw[1:2, :]) + a * cw[2:3, :])
    a_ref[...] = a
    _ffn_tail(conv, b, wd_ref, acc_scr, x_ref, g3_ref, o_ref)


def _ffn_common_specs(d, dff, tf):
    nf = dff // tf
    return [
        pl.BlockSpec((1, d), lambda i, j: (0, 0)),
        pl.BlockSpec((1, d), lambda i, j: (0, 0)),
        pl.BlockSpec((d, tf), lambda i, j: (0, j)),
        pl.BlockSpec((d, tf), lambda i, j: (0, j + nf)),
        pl.BlockSpec((3, tf), lambda i, j: (0, j)),
        pl.BlockSpec((1, tf), lambda i, j: (0, j)),
        pl.BlockSpec((tf, d), lambda i, j: (j, 0)),
    ]


def _ffn_prompt(x, seq_len, g2, g3, w_up, conv_w, conv_b, w_down, *, tm=512, tf=512):
    n, d = x.shape
    dff = w_down.shape[0]
    nt = n // tm
    hr = BF16_ROWS
    in_specs = [
        pl.BlockSpec((tm, d), lambda i, j: (i, 0)),
        pl.BlockSpec((hr, d), lambda i, j: (jnp.maximum(i * (tm // hr) - 1, 0), 0)),
    ] + _ffn_common_specs(d, dff, tf)
    y, tail = pl.pallas_call(
        functools.partial(_ffn_prompt_kernel, tiles_per_seq=seq_len // tm),
        out_shape=[jax.ShapeDtypeStruct((n, d), F32), jax.ShapeDtypeStruct((nt * 8, dff), F32)],
        grid=(nt, dff // tf),
        in_specs=in_specs,
        out_specs=[pl.BlockSpec((tm, d), lambda i, j: (i, 0)), pl.BlockSpec((8, tf), lambda i, j: (i, j))],
        scratch_shapes=[pltpu.VMEM((tm + hr, d), BF16), pltpu.VMEM((tm, d), F32)],
        compiler_params=_cparams("parallel", "arbitrary"),
        name="ffn_prompt",
    )(x, x, g2.reshape(1, d), g3.reshape(1, d), w_up, w_up, conv_w, conv_b.reshape(1, dff), w_down)
    return y, tail


def _ffn_sample(x, seq, p1, p2, g2, g3, w_up, conv_w, conv_b, w_down, *, tm=512, tf=512):
    n, d = x.shape
    dff = w_down.shape[0]
    in_specs = [
        pl.BlockSpec((tm, d), lambda i, j: (i, 0)),
        pl.BlockSpec((tm, tf), lambda i, j: (i, j)),
        pl.BlockSpec((tm, tf), lambda i, j: (i, j)),
    ] + _ffn_common_specs(d, dff, tf)
    y, a = pl.pallas_call(
        functools.partial(_ffn_sample_kernel, seq=seq),
        out_shape=[jax.ShapeDtypeStruct((n, d), F32), jax.ShapeDtypeStruct((n, dff), F32)],
        grid=(n // tm, dff // tf),
        in_specs=in_specs,
        out_specs=[pl.BlockSpec((tm, d), lambda i, j: (i, 0)), pl.BlockSpec((tm, tf), lambda i, j: (i, j))],
        scratch_shapes=[pltpu.VMEM((tm, d), BF16), pltpu.VMEM((tm, d), F32)],
        compiler_params=_cparams("parallel", "arbitrary"),
        name="ffn_sample",
    )(x, p1, p2, g2.reshape(1, d), g3.reshape(1, d), w_up, w_up, conv_w, conv_b.reshape(1, dff), w_down)
    return y, a


def _attn_prompt_kernel(q_ref, kc_ref, vc_ref, o_ref, l_ref, kp_scr, vp_scr):
    i = pl.program_id(2)
    tq = q_ref.shape[0]

    @pl.when(i == 0)
    def _():
        kp_scr[...] = jnp.zeros_like(kp_scr)
        vp_scr[...] = jnp.zeros_like(vp_scr)

    qi = lax.broadcasted_iota(jnp.int32, (tq, 2 * tq), 0) + tq
    ki = lax.broadcasted_iota(jnp.int32, (tq, 2 * tq), 1)
    dist = qi - ki
    first_key = jnp.where(i > 0, 0, tq)
    valid = (dist >= 0) & (dist <= A_STEPS) & (ki >= first_key)
    scale = A_HEAD_DIM ** -0.5
    low = lax.broadcasted_iota(jnp.int32, (tq, LANES), 1) < A_HEAD_DIM
    for pair in range(A_WIDTH // LANES):
        cs = slice(pair * LANES, (pair + 1) * LANES)
        q_pair = q_ref[:, cs]
        k_cur = kc_ref[:, cs].astype(BF16)
        v_cur = vc_ref[:, cs].astype(BF16)
        k = jnp.concatenate([kp_scr[:, cs], k_cur], axis=0)
        v = jnp.concatenate([vp_scr[:, cs], v_cur], axis=0)
        kp_scr[:, cs] = k_cur
        vp_scr[:, cs] = v_cur
        outs, lses = [], []
        for q in (jnp.where(low, q_pair, 0.0), jnp.where(low, 0.0, q_pair)):
            s = jnp.where(valid, _dot_nt(q.astype(BF16), k) * scale, NEG_INF)
            m = jnp.max(s, axis=-1, keepdims=True)
            p = jnp.exp(s - m)
            l = jnp.sum(p, axis=-1, keepdims=True)
            outs.append(_dot(p.astype(BF16), v) / l)
            lses.append(m + jnp.log(l))
        o_ref[:, cs] = jnp.where(low, outs[0], outs[1])
        l_ref[:, cs] = jnp.where(low, lses[0], lses[1])


def _attn_prompt(qkv, batch, seq):
    dil, n = qkv.shape[1], qkv.shape[2]
    tq = A_STEPS

    def spec(role):
        return pl.BlockSpec((None, None, tq, A_WIDTH), lambda b, c, i: (b, c, i, role))

    return pl.pallas_call(
        _attn_prompt_kernel,
        out_shape=[jax.ShapeDtypeStruct((batch, dil, n, A_WIDTH), F32)] * 2,
        grid=(batch, dil, n // tq),
        in_specs=[spec(0), spec(1), spec(2)],
        out_specs=[spec(0), spec(0)],
        scratch_shapes=[pltpu.VMEM((tq, A_WIDTH), BF16), pltpu.VMEM((tq, A_WIDTH), BF16)],
        compiler_params=_cparams("arbitrary", "arbitrary", "arbitrary"),
        name="attn_prompt_d%d" % dil,
    )(qkv, qkv, qkv)


def _attn_sample_kernel(new_ref, c0_ref, c1_ref, c2_ref, o_ref):
    nrow = new_ref.shape[2]
    s_new = nrow // A_HEADS
    scale = A_HEAD_DIM ** -0.5
    row_n = lax.broadcasted_iota(jnp.int32, (nrow, nrow), 0)
    col_n = lax.broadcasted_iota(jnp.int32, (nrow, nrow), 1)
    age_n = (row_n & (s_new - 1)) - (col_n & (s_new - 1))
    s_shift = s_new.bit_length() - 1
    same_head_n = (row_n >> s_shift) == (col_n >> s_shift)

    crefs = (c0_ref, c1_ref, c2_ref)
    chains = [(g, h) for g in range(len(A_GROUPS)) for h in range(A_HEADS)]
    qfs, vns, s_owns, valids = [], [], [], []
    for g, ((_, dil), cref) in enumerate(zip(A_GROUPS, crefs)):
        n_past = cref.shape[3]
        qfs.append(new_ref[g, 0])
        vns.append(new_ref[g, 2].astype(BF16))
        valid_n = same_head_n & (age_n >= 0) & ((age_n & (dil - 1)) == 0)
        s_owns.append(jnp.where(valid_n, _dot_nt(qfs[g].astype(BF16), new_ref[g, 1].astype(BF16)) * scale, NEG_INF))
        age = (lax.broadcasted_iota(jnp.int32, (s_new, n_past), 0)
               - lax.broadcasted_iota(jnp.int32, (s_new, n_past), 1))
        valids.append((age <= 0) & ((age & (dil - 1)) == 0))

    def rows(h):
        return slice(h * s_new, (h + 1) * s_new)

    s_bufs = [jnp.where(valids[g], _dot(qfs[g][rows(h), :].astype(BF16), crefs[g][0, h].astype(BF16)) * scale,
                        NEG_INF) for g, h in chains]
    ms = [jnp.maximum(jnp.max(s_buf, axis=-1, keepdims=True), jnp.max(s_owns[g][rows(h), :], axis=-1, keepdims=True))
          for (g, h), s_buf in zip(chains, s_bufs)]
    p_bufs = [jnp.exp(s_buf - m) for s_buf, m in zip(s_bufs, ms)]
    p_owns = [jnp.exp(s_owns[g][rows(h), :] - m) for (g, h), m in zip(chains, ms)]
    ls = [jnp.sum(pb, axis=-1, keepdims=True) + jnp.sum(po, axis=-1, keepdims=True) for pb, po in zip(p_bufs, p_owns)]
    os = [_dot_nt(pb.astype(BF16), crefs[g][1, h].astype(BF16)) + _dot(po.astype(BF16), vns[g])
          for (g, h), pb, po in zip(chains, p_bufs, p_owns)]
    outs, lses = [], []
    for g in range(len(A_GROUPS)):
        idx = [i for i, (gg, _) in enumerate(chains) if gg == g]
        outs.append(jnp.concatenate([os[i] / ls[i] for i in idx], axis=0))
        lses.append(jnp.concatenate([ms[i] + jnp.log(ls[i]) for i in idx], axis=0))

    mx = jnp.maximum(jnp.maximum(lses[0], lses[1]), lses[2])
    es = [jnp.exp(x - mx) for x in lses]
    den = es[0] + es[1] + es[2]
    o_ref[...] = (es[0] / den) * outs[0] + (es[1] / den) * outs[1] + (es[2] / den) * outs[2]


def _attn_sample(qkv, caches, j, batch, s_new):
    new = qkv.reshape(batch, s_new, len(A_GROUPS), 3, A_HEADS, A_HEAD_DIM)
    new = new.transpose(0, 2, 3, 4, 1, 5).reshape(batch, len(A_GROUPS), 3, A_HEADS * s_new, A_HEAD_DIM)
    views = []
    specs = [pl.BlockSpec((None,) + new.shape[1:], lambda b: (b, 0, 0, 0, 0))]
    for (win, dil), cache in zip(A_GROUPS, caches):
        n_past = cache.shape[2]
        assert n_past == dil * A_STEPS
        views.append(cache.transpose(0, 1, 3, 4, 5, 2))
        specs.append(pl.BlockSpec((None, None, 2, A_HEADS, A_HEAD_DIM, n_past), lambda b: (j, b, 0, 0, 0, 0)))
    out = pl.pallas_call(
        _attn_sample_kernel,
        out_shape=jax.ShapeDtypeStruct((batch, A_HEADS * s_new, A_HEAD_DIM), F32),
        grid=(batch,),
        in_specs=specs,
        out_specs=pl.BlockSpec((None, A_HEADS * s_new, A_HEAD_DIM), lambda b: (b, 0, 0)),
        compiler_params=_cparams("parallel"),
        name="attn_sample",
    )(new, *views)
    out = out.reshape(batch, A_HEADS, s_new, A_HEAD_DIM).transpose(0, 2, 1, 3)
    return out.reshape(batch * s_new, A_WIDTH)


def _retention_kernel(q_ref, k_ref, v_ref, g_ref, s0_ref, dec_ref, inn_ref, tail_ref, cd_ref,
                      o_ref, sout_ref, s_scr):
    c = pl.program_id(1)
    lk = dec_ref.shape[2]

    @pl.when(c == 0)
    def _():
        s_scr[...] = s0_ref[...]

    heads = range(R_HEADS)
    ks = [slice(h * R_DK, (h + 1) * R_DK) for h in heads]
    vs = [slice(h * R_DV, (h + 1) * R_DV) for h in heads]
    q = [q_ref[:, ks[h]].astype(BF16) for h in heads]
    kf = [_pad_rows(k_ref[:, ks[h]] * (R_DK ** -0.5), lk) for h in heads]
    vb = [_pad_rows(v_ref[:, vs[h]], lk).astype(BF16) for h in heads]
    state = [s_scr[h] for h in heads]
    s = [_dot_nt(q[h], kf[h].astype(BF16)) * dec_ref[h] for h in heads]
    cross = [_dot(q[h], state[h].astype(BF16)) * inn_ref[h] for h in heads]
    o = [_dot(s[h].astype(BF16), vb[h]) + cross[h] for h in heads]
    for h in heads:
        s_scr[h] = state[h] * cd_ref[h] + _dot_tn((kf[h] * tail_ref[h]).astype(BF16), vb[h])
    for h in heads:
        o_ref[:, vs[h]] = (jax.nn.silu(g_ref[:, vs[h]]) * _rms(o[h])).astype(o_ref.dtype)

    @pl.when(c == pl.num_programs(1) - 1)
    def _():
        sout_ref[...] = s_scr[...]


def _retention(proj, state, batch, seq):
    chunk = CHUNK if seq % CHUNK == 0 else seq
    lk = max(chunk, LANES)
    nc = seq // chunk
    nq, nv = R_HEADS * R_DK, R_HEADS * R_DV
    log_g = jnp.log1p(-jnp.exp2(-5.0 - jnp.arange(R_HEADS, dtype=F32)))
    i = jnp.arange(chunk, dtype=F32)
    diff = i[:, None] - i[None, :]
    decay = jnp.where(diff >= 0, jnp.exp(jnp.maximum(diff, 0.0)[None] * log_g[:, None, None]), 0.0)
    decay = jnp.pad(decay, ((0, 0), (0, 0), (0, lk - chunk)))
    inner = jnp.exp((i + 1.0)[None, :] * log_g[:, None])[:, :, None]
    tail = jnp.exp((chunk - 1.0 - i)[None, :] * log_g[:, None])
    tail = jnp.pad(tail, ((0, 0), (0, lk - chunk)))[:, :, None]
    cdec = jnp.exp(chunk * log_g)[:, None, None]
    out_dtype = BF16 if chunk % BF16_ROWS == 0 else F32
    whole = lambda b, c: (0, 0, 0)
    gated, new_state = pl.pallas_call(
        _retention_kernel,
        out_shape=[jax.ShapeDtypeStruct((batch * seq, nv), out_dtype),
                   jax.ShapeDtypeStruct(state.shape, F32)],
        grid=(batch, nc),
        in_specs=[
            pl.BlockSpec((chunk, nq), lambda b, c: (b * nc + c, 0)),
            pl.BlockSpec((chunk, nq), lambda b, c: (b * nc + c, 1)),
            pl.BlockSpec((chunk, nv), lambda b, c: (b * nc + c, 1)),
            pl.BlockSpec((chunk, nv), lambda b, c: (b * nc + c, 2)),
            pl.BlockSpec((None, R_HEADS, R_DK, R_DV), lambda b, c: (b, 0, 0, 0)),
            pl.BlockSpec((R_HEADS, chunk, lk), whole),
            pl.BlockSpec((R_HEADS, chunk, 1), whole),
            pl.BlockSpec((R_HEADS, lk, 1), whole),
            pl.BlockSpec((R_HEADS, 1, 1), whole),
        ],
        out_specs=[pl.BlockSpec((chunk, nv), lambda b, c: (b * nc + c, 0)),
                   pl.BlockSpec((None, R_HEADS, R_DK, R_DV), lambda b, c: (b, 0, 0, 0))],
        scratch_shapes=[pltpu.VMEM((R_HEADS, R_DK, R_DV), F32)],
        compiler_params=_cparams("parallel", "arbitrary"),
        name="retention",
    )(proj, proj, proj, proj, state, decay, inner, tail, cdec)
    return gated, new_state


def _mlstm_kernel(q_ref, k_ref, v_ref, og_ref, gt_ref, bias_ref, ng_ref, c0_ref, n0_ref, m0_ref,
                  o_ref, cout_ref, nout_ref, mout_ref, c_scr, n_scr, m_scr):
    c = pl.program_id(1)
    chunk = q_ref.shape[0]
    lk = max(chunk, LANES)

    @pl.when(c == 0)
    def _():
        c_scr[...] = c0_ref[...]
        n_scr[...] = n0_ref[...]
        m_scr[...] = m0_ref[...]

    pre = gt_ref[...] + bias_ref[...]
    log_f = -(jnp.maximum(-pre, 0.0) + jnp.log1p(jnp.exp(-jnp.abs(pre))))
    rowi = lax.broadcasted_iota(jnp.int32, pre.shape, 0)
    lane = lax.broadcasted_iota(jnp.int32, pre.shape, 1)
    cum = log_f
    d = 1
    while d < chunk:
        cum = cum + jnp.where(rowi >= d, pltpu.roll(cum, d, 0), 0.0)
        d *= 2
    cols = jnp.where(lane < M_HEADS, pre, cum)
    rows = _pad_rows(cols, lk).T
    t_idx = lax.broadcasted_iota(jnp.int32, (chunk, lk), 0)
    s_idx = lax.broadcasted_iota(jnp.int32, (chunk, lk), 1)
    causal = s_idx <= t_idx

    heads = range(M_HEADS)
    ks = [slice(h * M_DK, (h + 1) * M_DK) for h in heads]
    vs = [slice(h * M_DV, (h + 1) * M_DV) for h in heads]
    i_col = [cols[:, h:h + 1] for h in heads]
    b_col = [cols[:, M_HEADS + h:M_HEADS + h + 1] for h in heads]
    logw = [jnp.where(causal, (b_col[h] - rows[M_HEADS + h:M_HEADS + h + 1, :]) + rows[h:h + 1, :], NEG_INF)
            for h in heads]
    inter = [b_col[h] + m_scr[h:h + 1, 0:1] for h in heads]
    m_t = [jnp.maximum(inter[h], jnp.max(logw[h], axis=-1, keepdims=True)) for h in heads]
    w = [jnp.exp(logw[h] - m_t[h]) for h in heads]
    a = [jnp.exp(inter[h] - m_t[h]) for h in heads]

    qf = [q_ref[:, ks[h]] for h in heads]
    q = [qf[h].astype(BF16) for h in heads]
    kf = [k_ref[:, ks[h]] * (M_DK ** -0.5) for h in heads]
    vb = [_pad_rows(v_ref[:, vs[h]], lk).astype(BF16) for h in heads]
    cmat = [c_scr[h] for h in heads]
    nrow = [n_scr[h:h + 1, :] for h in heads]
    qk = [_dot_nt(q[h], _pad_rows(kf[h], lk).astype(BF16)) * w[h] for h in heads]
    cross = [a[h] * _dot(q[h], cmat[h].astype(BF16)) for h in heads]
    num = [_dot(qk[h].astype(BF16), vb[h]) + cross[h] for h in heads]
    den = [jnp.sum(qk[h], axis=-1, keepdims=True) + a[h] * jnp.sum(qf[h] * nrow[h], axis=-1, keepdims=True)
           for h in heads]
    hid = [num[h] / jnp.maximum(jnp.abs(den[h]), jnp.exp(-m_t[h])) for h in heads]

    m_end = [m_t[h][chunk - 1:chunk, :] for h in heads]
    a_end = [a[h][chunk - 1:chunk, :] for h in heads]
    kw = [kf[h] * jnp.exp(((b_col[h][chunk - 1:chunk, :] - b_col[h]) + i_col[h]) - m_end[h]) for h in heads]
    for h in heads:
        c_scr[h] = a_end[h] * cmat[h] + _dot_tn(_pad_rows(kw[h], lk).astype(BF16), vb[h])
        n_scr[h:h + 1, :] = a_end[h] * nrow[h] + jnp.sum(kw[h], axis=0, keepdims=True)
        m_scr[h:h + 1, :] = jnp.broadcast_to(m_end[h], (1, m_scr.shape[1]))
    for h in heads:
        y = _rms(hid[h]) * ng_ref[h:h + 1, :]
        o_ref[:, vs[h]] = (jax.nn.sigmoid(og_ref[:, vs[h]]) * y).astype(o_ref.dtype)

    @pl.when(c == pl.num_programs(1) - 1)
    def _():
        cout_ref[...] = c_scr[...]
        nout_ref[...] = n_scr[...]
        mout_ref[...] = m_scr[...]


def _mlstm(proj, gates, bias_row, norm_g, c0, n0, m0, batch, seq):
    chunk = CHUNK if seq % CHUNK == 0 else seq
    nc = seq // chunk
    nq, nv = M_HEADS * M_DK, M_HEADS * M_DV
    m0b = jnp.broadcast_to(m0[:, :, None], (batch, M_HEADS, LANES))
    out_dtype = BF16 if chunk % BF16_ROWS == 0 else F32
    st4 = lambda b, c: (b, 0, 0, 0)
    st3 = lambda b, c: (b, 0, 0)
    gated, c_new, n_new, m_new = pl.pallas_call(
        _mlstm_kernel,
        out_shape=[jax.ShapeDtypeStruct((batch * seq, nv), out_dtype),
                   jax.ShapeDtypeStruct(c0.shape, F32),
                   jax.ShapeDtypeStruct(n0.shape, F32),
                   jax.ShapeDtypeStruct(m0b.shape, F32)],
        grid=(batch, nc),
        in_specs=[
            pl.BlockSpec((chunk, nq), lambda b, c: (b * nc + c, 0)),
            pl.BlockSpec((chunk, nq), lambda b, c: (b * nc + c, 1)),
            pl.BlockSpec((chunk, nv), lambda b, c: (b * nc + c, 1)),
            pl.BlockSpec((chunk, nv), lambda b, c: (b * nc + c, 2)),
            pl.BlockSpec((chunk, LANES), lambda b, c: (b * nc + c, 0)),
            pl.BlockSpec((1, LANES), lambda b, c: (0, 0)),
            pl.BlockSpec((M_HEADS, M_DV), lambda b, c: (0, 0)),
            pl.BlockSpec((None, M_HEADS, M_DK, M_DV), st4),
            pl.BlockSpec((None, M_HEADS, M_DK), st3),
            pl.BlockSpec((None, M_HEADS, LANES), st3),
        ],
        out_specs=[pl.BlockSpec((chunk, nv), lambda b, c: (b * nc + c, 0)),
                   pl.BlockSpec((None, M_HEADS, M_DK, M_DV), st4),
                   pl.BlockSpec((None, M_HEADS, M_DK), st3),
                   pl.BlockSpec((None, M_HEADS, LANES), st3)],
        scratch_shapes=[pltpu.VMEM((M_HEADS, M_DK, M_DV), F32),
                        pltpu.VMEM((M_HEADS, M_DK), F32),
                        pltpu.VMEM((M_HEADS, LANES), F32)],
        compiler_params=_cparams("parallel", "arbitrary"),
        name="mlstm",
    )(proj, proj, proj, proj, gates, bias_row, norm_g, c0, n0, m0b)
    return gated, c_new, n_new, m_new[:, :, 0]


def _rope_tables_a(pos):
    inv = ROPE_THETA ** (-jnp.arange(ROPE_DIM // 2, dtype=F32) * (2.0 / ROPE_DIM))
    ang = pos.astype(F32)[:, None] * inv[None, :]
    cos, sin = jnp.cos(ang), jnp.sin(ang)
    half = ROPE_DIM // 2
    n = pos.shape[0]
    pad = A_HEAD_DIM - ROPE_DIM
    c = jnp.concatenate([cos, cos, jnp.ones((n, pad), F32)], axis=1)
    sp = jnp.concatenate([jnp.zeros((n, half), F32), sin, jnp.zeros((n, pad), F32)], axis=1)
    sm = jnp.concatenate([-sin, jnp.zeros((n, half + pad), F32)], axis=1)
    reps = LANES // A_HEAD_DIM
    return tuple(jnp.tile(t, (1, reps)) for t in (c, sp, sm))


def _rope_tables_b(pos):
    inv = R_THETA ** (-jnp.linspace(0.0, 1.0, R_DK // 2, dtype=F32))
    ang = pos.astype(F32)[:, None] * inv[None, :]
    return jnp.cos(ang), jnp.sin(ang)


def kernel(x_prompt, x_sample, cache_win128_kv, cache_win512_kv, cache_win2048_kv, state_ret,
           state_mlstm_c, state_mlstm_n, state_mlstm_m, state_ffn_conv, norm_gains,
           a_w_in, a_w_out, b_w_in, b_w_out, c_w_in, c_b_if, c_norm, c_w_out,
           f_w_up, f_conv_w, f_conv_b, f_w_down):
    bp, t, d = x_prompt.shape
    bs, s, _ = x_sample.shape
    depth = norm_gains.shape[0]
    caches = (cache_win128_kv, cache_win512_kv, cache_win2048_kv)
    pos_p = jnp.arange(t, dtype=jnp.int32)
    pos_s = jnp.tile(PAST_LEN + jnp.arange(s, dtype=jnp.int32), bs)
    tabs_a_p, tabs_a_s = _rope_tables_a(pos_p), _rope_tables_a(pos_s)
    tabs_b_p, tabs_b_s = _rope_tables_b(pos_p), _rope_tables_b(pos_s)
    tm = 512
    tm_p = 1024
    tm_f, tf_f = 1024, 512

    xp = x_prompt.reshape(bp * t, d)
    xs = x_sample.reshape(bs * s, d)
    win_p = [[] for _ in A_GROUPS]
    win_s = [[] for _ in A_GROUPS]
    ret_p, ret_s = [], []
    c_p, c_s, n_p, n_s, m_p, m_s = [], [], [], [], [], []
    conv_p, conv_s = [], []

    for layer in range(depth):
        kind, j = layer % N_MIXERS, layer // N_MIXERS
        g = norm_gains[layer]
        if kind == 0:
            w_in = a_w_in[j].astype(BF16)
            w_out = a_w_out[j].astype(BF16)
            qkv_p = _proj_a_prompt(xp, g[0], w_in, tabs_a_p, bp, t, tm=tm_p)
            qkv_s = _norm_proj(xs, g[0], w_in, mode="rope_a", tabs=tabs_a_s, tab_blocks=bs * s // tm, tm=tm)
            outs, lses = [], []
            for gi, (win, dil) in enumerate(A_GROUPS):
                o, l = _attn_prompt(qkv_p[gi], bp, t)
                outs.append(o)
                lses.append(l)
                keep = min(win, t)
                kv = qkv_p[gi][:, :, (t - keep) // dil:, A_WIDTH:].transpose(0, 2, 1, 3)
                win_p[gi].append(kv.reshape(bp, keep, 2, A_HEADS, A_HEAD_DIM))
                lo = gi * 3 * A_WIDTH + A_WIDTH
                kv = qkv_s.reshape(bs, s, -1)[:, :, lo:lo + 2 * A_WIDTH]
                win_s[gi].append(kv.reshape(bs, s, 2, A_HEADS, A_HEAD_DIM))
            xp = _post_merge(outs, lses, w_out, xp, g[1], t, tm=tm)
            att_s = _attn_sample(qkv_s, caches, j, bs, s)
            xs = _post(att_s, w_out, xs, g[1], tm=tm)
        elif kind == 1:
            w_in = b_w_in[j].astype(BF16)
            w_out = b_w_out[j].astype(BF16)
            proj_p = _norm_proj(xp, g[0], w_in, mode="rope_b", tabs=tabs_b_p, tab_blocks=t // tm_p, tm=tm_p)
            proj_s = _norm_proj(xs, g[0], w_in, mode="rope_b", tabs=tabs_b_s, tab_blocks=bs * s // tm, tm=tm)
            gated_p, sp_new = _retention(proj_p, jnp.zeros((bp, R_HEADS, R_DK, R_DV), F32), bp, t)
            gated_s, ss_new = _retention(proj_s, state_ret[j], bs, s)
            ret_p.append(sp_new)
            ret_s.append(ss_new)
            xp = _post(gated_p, w_out, xp, g[1], tm=tm)
            xs = _post(gated_s, w_out, xs, g[1], tm=tm)
        else:
            n_main = 2 * M_HEADS * M_DK + 2 * M_HEADS * M_DV
            w_in = c_w_in[j].astype(BF16)
            w_g = jnp.pad(c_w_in[j][:, n_main:], ((0, 0), (0, LANES - 2 * M_HEADS))).astype(BF16)
            w_out = c_w_out[j].astype(BF16)
            bias_row = jnp.pad(c_b_if[j].reshape(1, 2 * M_HEADS), ((0, 0), (0, LANES - 2 * M_HEADS)))
            proj_p, gates_p = _norm_proj(xp, g[0], w_in, mode="gates", wg=w_g, tm=tm_p)
            proj_s, gates_s = _norm_proj(xs, g[0], w_in, mode="gates", wg=w_g, tm=tm)
            zc = jnp.zeros((bp, M_HEADS, M_DK, M_DV), F32)
            zn = jnp.zeros((bp, M_HEADS, M_DK), F32)
            zm = jnp.zeros((bp, M_HEADS), F32)
            gated_p, cp, np_, mp = _mlstm(proj_p, gates_p, bias_row, c_norm[j], zc, zn, zm, bp, t)
            gated_s, cs, ns, ms = _mlstm(proj_s, gates_s, bias_row, c_norm[j], state_mlstm_c[j],
                                         state_mlstm_n[j], state_mlstm_m[j], bs, s)
            c_p.append(cp); c_s.append(cs)
            n_p.append(np_); n_s.append(ns)
            m_p.append(mp); m_s.append(ms)
            xp = _post(gated_p, w_out, xp, g[1], tm=tm)
            xs = _post(gated_s, w_out, xs, g[1], tm=tm)

        w_up = f_w_up[layer].astype(BF16)
        w_down = f_w_down[layer].astype(BF16)
        xp, tail = _ffn_prompt(xp, t, g[2], g[3], w_up, f_conv_w[layer], f_conv_b[layer], w_down, tm=tm_f, tf=tf_f)
        tail = tail.reshape(bp, t // tm_f, 8, D_FF)
        conv_p.append(tail[:, -1, 6:, :])
        buf = state_ffn_conv[layer]
        zrow = jnp.zeros((bs, s - 1, D_FF), F32)
        p1 = jnp.concatenate([buf[:, 1:2], zrow], axis=1).reshape(bs * s, D_FF)
        p2 = jnp.concatenate([buf, zrow[:, 1:]], axis=1).reshape(bs * s, D_FF)
        xs, a_s = _ffn_sample(xs, s, p1, p2, g[2], g[3], w_up, f_conv_w[layer], f_conv_b[layer], w_down, tm=tm)
        conv_s.append(a_s.reshape(bs, s, D_FF)[:, s - 2:, :])

    return (xp.reshape(bp, t, d), xs.reshape(bs, s, d),
            jnp.stack(win_p[0]), jnp.stack(win_s[0]),
            jnp.stack(win_p[1]), jnp.stack(win_s[1]),
            jnp.stack(win_p[2]), jnp.stack(win_s[2]),
            jnp.stack(ret_p), jnp.stack(ret_s),
            jnp.stack(c_p), jnp.stack(c_s),
            jnp.stack(n_p), jnp.stack(n_s),
            jnp.stack(m_p), jnp.stack(m_s),
            jnp.stack(conv_p), jnp.stack(conv_s))
```
